```python
import math
import jax, jax.numpy as jnp
from jax import lax
import numpy as np


D_MODEL = 2048
BATCH = 4
SEQ = 2048
DEPTH = 1
DEC_BATCH = 16
DEC_SEQ = 2048
PAST_LEN = 128

HEAD_DIM = 128
N_HEADS_A = 6
N_KV_A = 2
GQA_GROUP = N_HEADS_A // N_KV_A
DIL_PAIRS = ((128, 1), (512, 4), (2048, 16))
N_DIL = len(DIL_PAIRS)
HEADS_PER_DIL = 2
N_HEADS_B = N_DIL * HEADS_PER_DIL
N_HEADS_M = 4
N_MEM = 256
WIDTH_A = N_HEADS_A * HEAD_DIM
WIDTH_B = N_HEADS_B * HEAD_DIM
WIDTH_M = N_HEADS_M * HEAD_DIM
MIX_WIDTH = WIDTH_A + WIDTH_B + WIDTH_M
IN_SIZES = (WIDTH_A, N_KV_A * HEAD_DIM, N_KV_A * HEAD_DIM, WIDTH_B, WIDTH_B, WIDTH_B, WIDTH_M)
IN_COLS = sum(IN_SIZES)
IN_SPLITS = tuple(int(c) for c in np.cumsum(IN_SIZES)[:-1])
D_FF = 4 * D_MODEL
GRID_W = 64
ROPE_THETA = 10000.0
ROPE_AXIS_DIM = HEAD_DIM // 2
ROPE_PAIRS = ROPE_AXIS_DIM // 2
NUM_BUCKETS = 32
MAX_DISTANCE = 1024
Q_BLOCK = 128
EPS = 1e-6
NEG_INF = -1e30
ATTN_SCALE = HEAD_DIM ** -0.5

kernel_name = "hymba_parallel_dilated_encoder"


def rms_norm(x, g):
    xf = x.astype(jnp.float32)
    y = xf * lax.rsqrt(jnp.mean(jnp.square(xf), axis=-1, keepdims=True) + EPS)
    return (y * g.astype(jnp.float32)).astype(x.dtype)


def t5_bucket(rel):
    nb = NUM_BUCKETS // 2
    max_exact = nb // 2
    base = jnp.where(rel > 0, nb, 0)
    n = jnp.abs(rel)
    nf = jnp.maximum(n, 1).astype(jnp.float32)
    large = max_exact + (jnp.log(nf / max_exact) / math.log(MAX_DISTANCE / max_exact)
                         * (nb - max_exact)).astype(jnp.int32)
    large = jnp.minimum(large, nb - 1)
    return base + jnp.where(n < max_exact, n, large)


def axial_rope_tables(seq_len):
    rows = seq_len // GRID_W
    row = jnp.repeat(jnp.arange(rows), GRID_W).astype(jnp.float32)
    col = jnp.tile(jnp.arange(GRID_W), rows).astype(jnp.float32)
    inv = ROPE_THETA ** (-(2.0 * jnp.arange(ROPE_PAIRS, dtype=jnp.float32)) / ROPE_AXIS_DIM)
    ang_r = (row[:, None] * inv)[:, None, :]
    ang_c = (col[:, None] * inv)[:, None, :]
    return jnp.cos(ang_r), jnp.sin(ang_r), jnp.cos(ang_c), jnp.sin(ang_c)


def _rotate(a, cos, sin):
    a1, a2 = a[..., :ROPE_PAIRS], a[..., ROPE_PAIRS:]
    return jnp.concatenate([a1 * cos - a2 * sin, a2 * cos + a1 * sin], axis=-1)


def apply_axial_rope(x, cos_r, sin_r, cos_c, sin_c):
    xf = x.astype(jnp.float32)
    xr = _rotate(xf[..., :ROPE_AXIS_DIM], cos_r, sin_r)
    xc = _rotate(xf[..., ROPE_AXIS_DIM:], cos_c, sin_c)
    return jnp.concatenate([xr, xc], axis=-1).astype(x.dtype)


def gqa_attention(q, k, v):
    B, S = q.shape[0], q.shape[1]
    nblk = S // Q_BLOCK
    qg = q.reshape(B, nblk, Q_BLOCK, N_KV_A, GQA_GROUP, HEAD_DIM).transpose(1, 0, 2, 3, 4, 5)

    def block(qi):
        s = jnp.einsum("bqkgd,bskd->bkgqs", qi, k).astype(jnp.float32) * ATTN_SCALE
        p = jax.nn.softmax(s, axis=-1).astype(v.dtype)
        return jnp.einsum("bkgqs,bskd->bqkgd", p, v)

    o = lax.map(block, qg)
    return o.transpose(1, 0, 2, 3, 4, 5).reshape(B, S, WIDTH_A)


def dilated_group(q, k, v, bias_tab, window, dil):
    B, S, H, Dh = q.shape
    n_side = window // (2 * dil)
    band = n_side
    L = S // dil
    Lp = -(-L // band) * band
    nb = Lp // band

    def to_sub(x):
        x = x.reshape(B, L, dil, H, Dh).transpose(0, 2, 1, 3, 4)
        return jnp.pad(x, ((0, 0), (0, 0), (0, Lp - L), (0, 0), (0, 0)))

    def to_band(x):
        xp = jnp.pad(x, ((0, 0), (0, 0), (band, band), (0, 0), (0, 0)))
        xp = xp.reshape(B, dil, nb + 2, band, H, Dh)
        return jnp.concatenate([xp[:, :, :-2], xp[:, :, 1:-1], xp[:, :, 2:]], axis=3)

    qs = to_sub(q).reshape(B, dil, nb, band, H, Dh)
    kb = to_band(to_sub(k))
    vb = to_band(to_sub(v))

    i = jnp.arange(band)
    j = jnp.arange(3 * band)
    rel = j[None, :] - band - i[:, None]
    key_m = jnp.arange(nb)[:, None, None] * band - band + j[None, None, :]
    valid = (jnp.abs(rel) <= n_side)[None] & (key_m >= 0) & (key_m < L)
    bias = bias_tab[t5_bucket(rel * dil)].astype(jnp.float32).transpose(2, 0, 1)

    s = jnp.einsum("bcnqhd,bcnkhd->bcnhqk", qs, kb).astype(jnp.float32) * ATTN_SCALE
    s = jnp.where(valid[None, None, :, None], s + bias[None, None, None], NEG_INF)
    lse = jax.nn.logsumexp(s, axis=-1)
    p = jnp.exp(s - lse[..., None]).astype(v.dtype)
    o = jnp.einsum("bcnhqk,bcnkhd->bcnqhd", p, vb)
    o = o.reshape(B, dil, Lp, H, Dh)[:, :, :L].transpose(0, 2, 1, 3, 4).reshape(B, S, H, Dh)
    lse = lse.transpose(0, 1, 2, 4, 3).reshape(B, dil, Lp, H)[:, :, :L]
    lse = lse.transpose(0, 2, 1, 3).reshape(B, S, H)
    return o, lse


def dilated_attention(q, k, v, rel_bias):
    B, S = q.shape[0], q.shape[1]
    shp = (B, S, N_DIL, HEADS_PER_DIL, HEAD_DIM)
    q, k, v = q.reshape(shp), k.reshape(shp), v.reshape(shp)
    tab = rel_bias.reshape(NUM_BUCKETS, N_DIL, HEADS_PER_DIL)
    outs, lses = [], []
    for g, (window, dil) in enumerate(DIL_PAIRS):
        o, lse = dilated_group(q[:, :, g], k[:, :, g], v[:, :, g], tab[:, g], window, dil)
        outs.append(o)
        lses.append(lse)
    alpha = jax.nn.softmax(jnp.stack(lses), axis=0)
    o = alpha[..., None] * jnp.stack(outs).astype(jnp.float32)
    return o.transpose(1, 2, 0, 3, 4).reshape(B, S, WIDTH_B).astype(q.dtype)


def memory_attention(q, mem, mem_norm, w_mem_kv):
    B, S = q.shape[0], q.shape[1]
    q = q.reshape(B, S, N_HEADS_M, HEAD_DIM)
    kv = (rms_norm(mem, mem_norm) @ w_mem_kv).reshape(B, mem.shape[1], 2, N_HEADS_M, HEAD_DIM)
    s = jnp.einsum("bshd,bmhd->bhsm", q, kv[:, :, 0]).astype(jnp.float32) * ATTN_SCALE
    p = jax.nn.softmax(s, axis=-1).astype(q.dtype)
    return jnp.einsum("bhsm,bmhd->bshd", p, kv[:, :, 1]).reshape(B, S, WIDTH_M)


def encoder_layer(x, mem, rel_bias, pre_mix_norm, w_in, q_norm_a, k_norm_a, mem_norm,
                  w_mem_kv, out_norm_a, out_norm_b, out_norm_m, w_out, post_mix_norm,
                  pre_ffn_norm, w_up, w_down, post_ffn_norm):
    B, S, _ = x.shape
    h = rms_norm(x, pre_mix_norm)
    qa, ka, va, qb, kb, vb, qm = jnp.split(h @ w_in, IN_SPLITS, axis=-1)

    cos_r, sin_r, cos_c, sin_c = axial_rope_tables(S)
    qa = apply_axial_rope(rms_norm(qa.reshape(B, S, N_HEADS_A, HEAD_DIM), q_norm_a), cos_r, sin_r, cos_c, sin_c)
    ka = apply_axial_rope(rms_norm(ka.reshape(B, S, N_KV_A, HEAD_DIM), k_norm_a), cos_r, sin_r, cos_c, sin_c)
    o_a = gqa_attention(qa, ka, va.reshape(B, S, N_KV_A, HEAD_DIM))

    o_b = dilated_attention(qb, kb, vb, rel_bias)

    o_m = memory_attention(qm, mem, mem_norm, w_mem_kv)

    mix = jnp.concatenate([rms_norm(o_a, out_norm_a), rms_norm(o_b, out_norm_b),
                           rms_norm(o_m, out_norm_m)], axis=-1)
    x = x + rms_norm(mix @ w_out, post_mix_norm)

    h = rms_norm(x, pre_ffn_norm)
    u = jnp.square(jax.nn.relu(h @ w_up))
    return x + rms_norm(u @ w_down, post_ffn_norm)


def setup_inputs(seed: int = 0) -> dict:
    key = jax.random.key(seed)
    ks = jax.random.split(key, 24)

    def nrm(k, shape, scale):
        return jax.random.normal(k, shape, jnp.float32) * scale

    def gain(k, shape):
        return 1.0 + 0.02 * jax.random.normal(k, shape, jnp.float32)

    return {
        "x_prompt": nrm(ks[0], (BATCH, SEQ, D_MODEL), 1.0),
        "x_sample": nrm(ks[1], (DEC_BATCH, DEC_SEQ, D_MODEL), 1.0),
        "mem_prompt": nrm(ks[2], (BATCH, N_MEM, D_MODEL), 1.0),
        "mem_sample": nrm(ks[3], (DEC_BATCH, N_MEM, D_MODEL), 1.0),
        "rel_bias": nrm(ks[4], (NUM_BUCKETS, N_HEADS_B), 0.5),
        "pre_mix_norm": gain(ks[5], (DEPTH, D_MODEL)),
        "w_in": nrm(ks[6], (DEPTH, D_MODEL, IN_COLS), D_MODEL ** -0.5),
        "q_norm_a": gain(ks[7], (DEPTH, HEAD_DIM)),
        "k_norm_a": gain(ks[8], (DEPTH, HEAD_DIM)),
        "mem_norm": gain(ks[9], (DEPTH, D_MODEL)),
        "w_mem_kv": nrm(ks[10], (DEPTH, D_MODEL, 2 * WIDTH_M), D_MODEL ** -0.5),
        "out_norm_a": gain(ks[11], (DEPTH, WIDTH_A)),
        "out_norm_b": gain(ks[12], (DEPTH, WIDTH_B)),
        "out_norm_m": gain(ks[13], (DEPTH, WIDTH_M)),
        "w_out": nrm(ks[14], (DEPTH, MIX_WIDTH, D_MODEL), MIX_WIDTH ** -0.5),
        "post_mix_norm": gain(ks[15], (DEPTH, D_MODEL)),
        "pre_ffn_norm": gain(ks[16], (DEPTH, D_MODEL)),
        "w_up": nrm(ks[17], (DEPTH, D_MODEL, D_FF), D_MODEL ** -0.5),
        "w_down": nrm(ks[18], (DEPTH, D_FF, D_MODEL), D_FF ** -0.5),
        "post_ffn_norm": gain(ks[19], (DEPTH, D_MODEL)),
    }


def reference(x_prompt, x_sample, mem_prompt, mem_sample, rel_bias, pre_mix_norm, w_in,
              q_norm_a, k_norm_a, mem_norm, w_mem_kv, out_norm_a, out_norm_b, out_norm_m,
              w_out, post_mix_norm, pre_ffn_norm, w_up, w_down, post_ffn_norm):
    def trunk(x, mem):
        for l in range(DEPTH):
            x = encoder_layer(x, mem, rel_bias, pre_mix_norm[l], w_in[l], q_norm_a[l],
                              k_norm_a[l], mem_norm[l], w_mem_kv[l], out_norm_a[l],
                              out_norm_b[l], out_norm_m[l], w_out[l], post_mix_norm[l],
                              pre_ffn_norm[l], w_up[l], w_down[l], post_ffn_norm[l])
        return x

    y_prompt = trunk(x_prompt, mem_prompt)
    y_sample = trunk(x_sample, mem_sample)
    return (y_prompt, y_sample)
```

```python
import functools
import math

import jax
import jax.numpy as jnp
import numpy as np
from jax import lax
from jax.experimental import pallas as pl
from jax.experimental.pallas import tpu as pltpu

F32 = jnp.float32
BF16 = jnp.bfloat16

HEAD_DIM = 128
N_HEADS_A = 6
N_KV_A = 2
GQA_GROUP = N_HEADS_A // N_KV_A
DIL_PAIRS = ((128, 1), (512, 4), (2048, 16))
N_DIL = len(DIL_PAIRS)
HEADS_PER_DIL = 2
N_HEADS_B = N_DIL * HEADS_PER_DIL
N_HEADS_M = 4
WIDTH_A = N_HEADS_A * HEAD_DIM
WIDTH_KV_A = N_KV_A * HEAD_DIM
WIDTH_B = N_HEADS_B * HEAD_DIM
WIDTH_M = N_HEADS_M * HEAD_DIM
GROUP_W = HEADS_PER_DIL * HEAD_DIM
COL_QA = 0
COL_KA = COL_QA + WIDTH_A
COL_VA = COL_KA + WIDTH_KV_A
COL_QB = COL_VA + WIDTH_KV_A
COL_KB = COL_QB + WIDTH_B
COL_VB = COL_KB + WIDTH_B
COL_QM = COL_VB + WIDTH_B
IN_COLS = COL_QM + WIDTH_M
ROPE_COLS = COL_VA
GRID_W = 64
ROPE_THETA = 10000.0
ROPE_AXIS_DIM = HEAD_DIM // 2
ROPE_PAIRS = ROPE_AXIS_DIM // 2
NUM_BUCKETS = 32
MAX_DISTANCE = 1024
EPS = 1e-6
NEG_INF = -1e30
ATTN_SCALE = HEAD_DIM ** -0.5

V7X_VMEM_BYTES = 64 * 1024 * 1024
VMEM_LIMIT_BYTES = V7X_VMEM_BYTES - 8 * 1024 * 1024


def _params(*semantics):
    return pltpu.CompilerParams(dimension_semantics=semantics,
                                vmem_limit_bytes=VMEM_LIMIT_BYTES)


def _rms(v, g):
    ms = jnp.mean(v * v, axis=-1, keepdims=True)
    return v * lax.rsqrt(ms + EPS) * g


def _softmax_pv(s, v):
    m = jnp.max(s, axis=-1, keepdims=True)
    p = jnp.exp(s - m)
    l = jnp.sum(p, axis=-1, keepdims=True)
    o = jnp.dot(p.astype(BF16), v, preferred_element_type=F32)
    return o, m, l


def _qk(q, k):
    return lax.dot_general(q, k, (((1,), (1,)), ((), ())), preferred_element_type=F32)


def _norm_matmul_kernel(x_ref, g_ref, w_ref, o_ref, h_scr):
    @pl.when(pl.program_id(1) == 0)
    def _():
        h_scr[...] = _rms(x_ref[...], g_ref[...]).astype(BF16)

    o_ref[...] = jnp.dot(h_scr[...], w_ref[...], preferred_element_type=F32).astype(o_ref.dtype)


def _in_proj_kernel(x_ref, g_ref, w_ref, hg_ref, cos_ref, sin_ref, o_ref, h_scr, *, rope_tiles):
    j = pl.program_id(1)

    @pl.when(j == 0)
    def _():
        h_scr[...] = _rms(x_ref[...], g_ref[...]).astype(BF16)

    acc = jnp.dot(h_scr[...], w_ref[...], preferred_element_type=F32)
    tm, tn = acc.shape

    @pl.when(j >= rope_tiles)
    def _():
        o_ref[...] = acc.astype(o_ref.dtype)

    @pl.when(j < rope_tiles)
    def _():
        cos = cos_ref[...]
        sin = sin_ref[...]
        lane = lax.broadcasted_iota(jnp.int32, (tm, HEAD_DIM), 1)
        low_half = (lane % ROPE_AXIS_DIM) < ROPE_PAIRS
        for h in range(tn // HEAD_DIM):
            cs = slice(h * HEAD_DIM, (h + 1) * HEAD_DIM)
            y = _rms(acc[:, cs], hg_ref[:, cs])
            partner = jnp.where(low_half, pltpu.roll(y, HEAD_DIM - ROPE_PAIRS, 1),
                                pltpu.roll(y, ROPE_PAIRS, 1))
            o_ref[:, cs] = (y * cos + partner * sin).astype(o_ref.dtype)


def _in_proj(x2d, gain, w, head_gain, cos_tab, sin_tab, *, seq, tm=1024, tn=1024):
    t, d = x2d.shape
    n = w.shape[1]
    assert t % tm == 0 and n % tn == 0 and seq % tm == 0 and ROPE_COLS % tn == 0
    pos_blocks = seq // tm
    kern = functools.partial(_in_proj_kernel, rope_tiles=ROPE_COLS // tn)
    return pl.pallas_call(
        kern,
        grid=(t // tm, n // tn),
        in_specs=[
            pl.BlockSpec((tm, d), lambda i, j: (i, 0)),
            pl.BlockSpec((1, d), lambda i, j: (0, 0)),
            pl.BlockSpec((d, tn), lambda i, j: (0, j)),
            pl.BlockSpec((1, tn), lambda i, j: (0, 0)),
            pl.BlockSpec((tm, HEAD_DIM), lambda i, j: (i % pos_blocks, 0)),
            pl.BlockSpec((tm, HEAD_DIM), lambda i, j: (i % pos_blocks, 0)),
        ],
        out_specs=pl.BlockSpec((tm, tn), lambda i, j: (i, j)),
        out_shape=jax.ShapeDtypeStruct((t, n), BF16),
        scratch_shapes=[pltpu.VMEM((tm, d), BF16)],
        compiler_params=_params("parallel", "arbitrary"),
        name="in_proj",
    )(x2d, gain, w, head_gain, cos_tab, sin_tab)


def _mem_kv(mem2d, gain, w, *, tm=1024, tn=1024):
    t, d = mem2d.shape
    n = w.shape[1]
    assert t % tm == 0 and n % tn == 0
    return pl.pallas_call(
        _norm_matmul_kernel,
        grid=(t // tm, n // tn),
        in_specs=[
            pl.BlockSpec((tm, d), lambda i, j: (i, 0)),
            pl.BlockSpec((1, d), lambda i, j: (0, 0)),
            pl.BlockSpec((d, tn), lambda i, j: (0, j)),
        ],
        out_specs=pl.BlockSpec((tm, tn), lambda i, j: (i, j)),
        out_shape=jax.ShapeDtypeStruct((t, n), BF16),
        scratch_shapes=[pltpu.VMEM((tm, d), BF16)],
        compiler_params=_params("parallel", "arbitrary"),
        name="mem_kv",
    )(mem2d, gain, w)


def _attn_a_kernel(q_ref, k_ref, v_ref, o_ref):
    k = k_ref[...]
    v = v_ref[...]
    for g in range(GQA_GROUP):
        cs = slice(g * HEAD_DIM, (g + 1) * HEAD_DIM)
        s = _qk(q_ref[:, cs], k) * ATTN_SCALE
        o, _, l = _softmax_pv(s, v)
        o_ref[:, cs] = (o / l).astype(o_ref.dtype)


def _attn_a(proj, *, tq=512):
    b, s, _ = proj.shape
    qw = GQA_GROUP * HEAD_DIM
    return pl.pallas_call(
        _attn_a_kernel,
        grid=(b, N_KV_A, s // tq),
        in_specs=[
            pl.BlockSpec((None, tq, qw), lambda bi, kv, i: (bi, i, COL_QA // qw + kv)),
            pl.BlockSpec((None, s, HEAD_DIM), lambda bi, kv, i: (bi, 0, COL_KA // HEAD_DIM + kv)),
            pl.BlockSpec((None, s, HEAD_DIM), lambda bi, kv, i: (bi, 0, COL_VA // HEAD_DIM + kv)),
        ],
        out_specs=pl.BlockSpec((None, tq, qw), lambda bi, kv, i: (bi, i, kv)),
        out_shape=jax.ShapeDtypeStruct((b, s, WIDTH_A), BF16),
        compiler_params=_params("parallel", "parallel", "arbitrary"),
        name="attn_a",
    )(proj, proj, proj)


B_TQ = 128
B_HALF = 64


def _band_window(sub_len):
    return min(sub_len, B_TQ + 2 * B_HALF)


def _band_blocks(sub_len):
    w = _band_window(sub_len)
    nblk = sub_len // B_TQ
    out = []
    for qb in range(nblk):
        a = qb * B_TQ
        ks = min(max(a - B_HALF, 0), sub_len - w)
        case = 0 if qb == 0 else (2 if qb == nblk - 1 else 1)
        out.append((a, ks, case))
    return out


def _attn_b_kernel(q_ref, k_ref, v_ref, bias_ref, o_ref, lse_ref, *, sub_len):
    w = _band_window(sub_len)
    for h in range(HEADS_PER_DIL):
        cs = slice(h * HEAD_DIM, (h + 1) * HEAD_DIM)
        for a, ks, case in _band_blocks(sub_len):
            q = q_ref[a:a + B_TQ, cs]
            k = k_ref[ks:ks + w, cs]
            v = v_ref[ks:ks + w, cs]
            s = _qk(q, k) * ATTN_SCALE + bias_ref[case, h]
            o, m, l = _softmax_pv(s, v)
            o_ref[a:a + B_TQ, cs] = (o / l).astype(o_ref.dtype)
            lse_ref[a:a + B_TQ, cs] = jnp.broadcast_to(m + jnp.log(l), (B_TQ, HEAD_DIM))


def _t5_bucket(rel):
    nb = NUM_BUCKETS // 2
    max_exact = nb // 2
    base = jnp.where(rel > 0, nb, 0)
    n = jnp.abs(rel)
    nf = jnp.maximum(n, 1).astype(F32)
    large = max_exact + (jnp.log(nf / max_exact) / math.log(MAX_DISTANCE / max_exact)
                         * (nb - max_exact)).astype(jnp.int32)
    large = jnp.minimum(large, nb - 1)
    return base + jnp.where(n < max_exact, n, large)


def _band_bias(tab, sub_len, dil):
    w = _band_window(sub_len)
    blocks = _band_blocks(sub_len)
    offsets = {}
    for a, ks, case in blocks:
        assert offsets.setdefault(case, ks - a) == ks - a
    i = np.arange(B_TQ)[:, None]
    j = np.arange(w)[None, :]
    tiles = []
    for case in sorted(offsets):
        rel = jnp.asarray(offsets[case] + j - i, jnp.int32)
        bias = tab[_t5_bucket(rel * dil)].astype(F32).transpose(2, 0, 1)
        tiles.append(jnp.where((jnp.abs(rel) <= B_HALF)[None], bias, NEG_INF))
    return jnp.stack(tiles)


def _attn_b_group(proj, bias, *, group, dil):
    b, s, cols = proj.shape
    sub_len = s // dil
    per_res = cols // GROUP_W
    view = proj.reshape(b, sub_len, dil * cols)

    def spec(col0):
        return pl.BlockSpec((None, sub_len, GROUP_W),
                            lambda bi, r: (bi, 0, r * per_res + col0 // GROUP_W + group))

    out_spec = pl.BlockSpec((None, sub_len, GROUP_W), lambda bi, r: (bi, 0, r))
    o, lse = pl.pallas_call(
        functools.partial(_attn_b_kernel, sub_len=sub_len),
        grid=(b, dil),
        in_specs=[spec(COL_QB), spec(COL_KB), spec(COL_VB),
                  pl.BlockSpec(bias.shape, lambda bi, r: (0, 0, 0, 0))],
        out_specs=[out_spec, out_spec],
        out_shape=[jax.ShapeDtypeStruct((b, sub_len, dil * GROUP_W), BF16),
                   jax.ShapeDtypeStruct((b, sub_len, dil * GROUP_W), F32)],
        compiler_params=_params("parallel", "arbitrary"),
        name=f"attn_b_dil{dil}",
    )(view, view, view, bias)
    return o.reshape(b * s, GROUP_W), lse.reshape(b * s, GROUP_W)


def _attn_m_kernel(q_ref, kv_ref, o_ref):
    for h in range(N_HEADS_M):
        cs = slice(h * HEAD_DIM, (h + 1) * HEAD_DIM)
        vs = slice(WIDTH_M + h * HEAD_DIM, WIDTH_M + (h + 1) * HEAD_DIM)
        s = _qk(q_ref[:, cs], kv_ref[:, cs]) * ATTN_SCALE
        o, _, l = _softmax_pv(s, kv_ref[:, vs])
        o_ref[:, cs] = (o / l).astype(o_ref.dtype)


def _attn_m(proj, kv, *, tq=1024):
    b, s, _ = proj.shape
    n_mem = kv.shape[1]
    return pl.pallas_call(
        _attn_m_kernel,
        grid=(b, s // tq),
        in_specs=[
            pl.BlockSpec((None, tq, WIDTH_M), lambda bi, i: (bi, i, COL_QM // WIDTH_M)),
            pl.BlockSpec((None, n_mem, 2 * WIDTH_M), lambda bi, i: (bi, 0, 0)),
        ],
        out_specs=pl.BlockSpec((None, tq, WIDTH_M), lambda bi, i: (bi, i, 0)),
        out_shape=jax.ShapeDtypeStruct((b, s, WIDTH_M), BF16),
        compiler_params=_params("parallel", "arbitrary"),
        name="attn_m",
    )(proj, kv)


def _out_proj_kernel(oa_ref, ob0_ref, ob1_ref, ob2_ref, l0_ref, l1_ref, l2_ref, om_ref, x_ref,
                     ga_ref, gb_ref, gm_ref, w_ref, gpost_ref, o_ref):
    na = _rms(oa_ref[...].astype(F32), ga_ref[...]).astype(BF16)
    l0, l1, l2 = l0_ref[...], l1_ref[...], l2_ref[...]
    mx = jnp.maximum(jnp.maximum(l0, l1), l2)
    e0, e1, e2 = jnp.exp(l0 - mx), jnp.exp(l1 - mx), jnp.exp(l2 - mx)
    den = e0 + e1 + e2
    ob = jnp.concatenate([(e0 / den) * ob0_ref[...].astype(F32),
                          (e1 / den) * ob1_ref[...].astype(F32),
                          (e2 / den) * ob2_ref[...].astype(F32)], axis=-1)
    nb = _rms(ob, gb_ref[...]).astype(BF16)
    nm = _rms(om_ref[...].astype(F32), gm_ref[...]).astype(BF16)
    mix = jnp.concatenate([na, nb, nm], axis=-1)
    y = jnp.dot(mix, w_ref[...], preferred_element_type=F32)
    o_ref[...] = x_ref[...] + _rms(y, gpost_ref[...])


def _out_proj(oa, obs, lses, om, x2d, ga, gb, gm, w, gpost, *, tm=512):
    t, d = x2d.shape

    def rows(width):
        return pl.BlockSpec((tm, width), lambda i: (i, 0))

    def whole(arr):
        return pl.BlockSpec(arr.shape, lambda i: (0, 0))

    return pl.pallas_call(
        _out_proj_kernel,
        grid=(t // tm,),
        in_specs=[rows(WIDTH_A)] + [rows(GROUP_W)] * 6 + [rows(WIDTH_M), rows(d),
                  whole(ga), whole(gb), whole(gm), whole(w), whole(gpost)],
        out_specs=rows(d),
        out_shape=jax.ShapeDtypeStruct((t, d), F32),
        compiler_params=_params("parallel"),
        name="out_proj",
    )(oa, *obs, *lses, om, x2d, ga, gb, gm, w, gpost)


def _ffn_kernel(x_ref, gpre_ref, wup_ref, wdn_ref, gpost_ref, o_ref, h_scr, acc_scr):
    j = pl.program_id(1)

    @pl.when(j == 0)
    def _():
        h_scr[...] = _rms(x_ref[...], gpre_ref[...]).astype(BF16)
        acc_scr[...] = jnp.zeros_like(acc_scr)

    u = jnp.dot(h_scr[...], wup_ref[...], preferred_element_type=F32)
    u = jnp.square(jnp.maximum(u, 0.0)).astype(BF16)
    acc_scr[...] += jnp.dot(u, wdn_ref[...], preferred_element_type=F32)

    @pl.when(j == pl.num_programs(1) - 1)
    def _():
        o_ref[...] = x_ref[...] + _rms(acc_scr[...], gpost_ref[...])


def _ffn(x2d, gpre, w_up, w_down, gpost, *, tm=512, tf=512):
    t, d = x2d.shape
    d_ff = w_up.shape[1]
    return pl.pallas_call(
        _ffn_kernel,
        grid=(t // tm, d_ff // tf),
        in_specs=[
            pl.BlockSpec((tm, d), lambda i, j: (i, 0)),
            pl.BlockSpec((1, d), lambda i, j: (0, 0)),
            pl.BlockSpec((d, tf), lambda i, j: (0, j)),
            pl.BlockSpec((tf, d), lambda i, j: (j, 0)),
            pl.BlockSpec((1, d), lambda i, j: (0, 0)),
        ],
        out_specs=pl.BlockSpec((tm, d), lambda i, j: (i, 0)),
        out_shape=jax.ShapeDtypeStruct((t, d), F32),
        scratch_shapes=[pltpu.VMEM((tm, d), BF16), pltpu.VMEM((tm, d), F32)],
        compiler_params=_params("parallel", "arbitrary"),
        name="ffn",
    )(x2d, gpre, w_up, w_down, gpost)


def _rope_tables(seq_len):
    rows = seq_len // GRID_W
    row = jnp.repeat(jnp.arange(rows), GRID_W).astype(F32)
    col = jnp.tile(jnp.arange(GRID_W), rows).astype(F32)
    inv = ROPE_THETA ** (-(2.0 * jnp.arange(ROPE_PAIRS, dtype=F32)) / ROPE_AXIS_DIM)
    ang_r = row[:, None] * inv
    ang_c = col[:, None] * inv
    cos = jnp.concatenate([jnp.cos(ang_r)] * 2 + [jnp.cos(ang_c)] * 2, axis=-1)
    sin = jnp.concatenate([-jnp.sin(ang_r), jnp.sin(ang_r), -jnp.sin(ang_c), jnp.sin(ang_c)], axis=-1)
    return cos, sin


def _row(v):
    return v.reshape(1, -1).astype(F32)


def _encoder_layer(x, mem, p):
    b, s, d = x.shape
    x2d = x.reshape(b * s, d)
    cos_tab, sin_tab = _rope_tables(s)
    proj = _in_proj(x2d, p["pre_mix_norm"], p["w_in"], p["head_gain"], cos_tab, sin_tab, seq=s)
    proj = proj.reshape(b, s, IN_COLS)
    kv = _mem_kv(mem.reshape(-1, d), p["mem_norm"], p["w_mem_kv"]).reshape(b, mem.shape[1], -1)

    oa = _attn_a(proj).reshape(b * s, WIDTH_A)
    obs, lses = [], []
    for g, (_, dil) in enumerate(DIL_PAIRS):
        bias = _band_bias(p["bias_tab"][:, g], s // dil, dil)
        o, lse = _attn_b_group(proj, bias, group=g, dil=dil)
        obs.append(o)
        lses.append(lse)
    om = _attn_m(proj, kv).reshape(b * s, WIDTH_M)

    x1 = _out_proj(oa, obs, lses, om, x2d, p["out_norm_a"], p["out_norm_b"], p["out_norm_m"],
                   p["w_out"], p["post_mix_norm"])
    y = _ffn(x1, p["pre_ffn_norm"], p["w_up"], p["w_down"], p["post_ffn_norm"])
    return y.reshape(b, s, d)


def kernel(x_prompt, x_sample, mem_prompt, mem_sample, rel_bias, pre_mix_norm, w_in, q_norm_a, k_norm_a, mem_norm, w_mem_kv, out_norm_a, out_norm_b, out_norm_m, w_out, post_mix_norm, pre_ffn_norm, w_up, w_down, post_ffn_norm):
    depth = w_in.shape[0]
    bias_tab = rel_bias.reshape(NUM_BUCKETS, N_DIL, HEADS_PER_DIL)
    layers = []
    for l in range(depth):
        layers.append({
            "bias_tab": bias_tab,
            "pre_mix_norm": _row(pre_mix_norm[l]),
            "w_in": w_in[l].astype(BF16),
            "head_gain": _row(jnp.concatenate([jnp.tile(q_norm_a[l], N_HEADS_A),
                                               jnp.tile(k_norm_a[l], N_KV_A)])),
            "mem_norm": _row(mem_norm[l]),
            "w_mem_kv": w_mem_kv[l].astype(BF16),
            "out_norm_a": _row(out_norm_a[l]),
            "out_norm_b": _row(out_norm_b[l]),
            "out_norm_m": _row(out_norm_m[l]),
            "w_out": w_out[l].astype(BF16),
            "post_mix_norm": _row(post_mix_norm[l]),
            "pre_ffn_norm": _row(pre_ffn_norm[l]),
            "w_up": w_up[l].astype(BF16),
            "w_down": w_down[l].astype(BF16),
            "post_ffn_norm": _row(post_ffn_norm[l]),
        })

    def trunk(x, mem):
        for p in layers:
            x = _encoder_layer(x, mem, p)
        return x

    return (trunk(x_prompt, mem_prompt), trunk(x_sample, mem_sample))
```

```python
import functools
import math

import jax
import jax.numpy as jnp
import numpy as np
from jax import lax
from jax.experimental import pallas as pl
from jax.experimental.pallas import tpu as pltpu

F32 = jnp.float32
BF16 = jnp.bfloat16

HEAD_DIM = 128
N_HEADS_A = 6
N_KV_A = 2
GQA_GROUP = N_HEADS_A // N_KV_A
DIL_PAIRS = ((128, 1), (512, 4), (2048, 16))
N_DIL = len(DIL_PAIRS)
HEADS_PER_DIL = 2
N_HEADS_B = N_DIL * HEADS_PER_DIL
N_HEADS_M = 4
WIDTH_A = N_HEADS_A * HEAD_DIM
WIDTH_KV_A = N_KV_A * HEAD_DIM
WIDTH_B = N_HEADS_B * HEAD_DIM
WIDTH_M = N_HEADS_M * HEAD_DIM
GROUP_W = HEADS_PER_DIL * HEAD_DIM
GRID_W = 64
ROPE_THETA = 10000.0
ROPE_AXIS_DIM = HEAD_DIM // 2
ROPE_PAIRS = ROPE_AXIS_DIM // 2
NUM_BUCKETS = 32
MAX_DISTANCE = 1024
EPS = 1e-6
NEG_INF = -1e30
ATTN_SCALE = HEAD_DIM ** -0.5
LOG2E = math.log2(math.e)

PA_QA = 0
PA_KA = PA_QA + WIDTH_A
ROPE_COLS = PA_KA + WIDTH_KV_A
PA_QM = ROPE_COLS
PA_VA = PA_QM + WIDTH_M
PA_B0 = PA_VA + WIDTH_KV_A
PA_COLS = PA_B0 + 3 * GROUP_W
PB_COLS = (N_DIL - 1) * 3 * GROUP_W
assert PA_QA % (GQA_GROUP * HEAD_DIM) == 0 and PA_QM % WIDTH_M == 0
IN_COLS = PA_COLS + PB_COLS
IN_CHUNK = 512

V7X_VMEM_BYTES = 64 * 1024 * 1024
VMEM_LIMIT_BYTES = V7X_VMEM_BYTES - 8 * 1024 * 1024


def _params(*semantics):
    return pltpu.CompilerParams(dimension_semantics=semantics,
                                vmem_limit_bytes=VMEM_LIMIT_BYTES)


def _resident(shape):
    zeros = (0,) * len(shape)
    return pl.BlockSpec(shape, lambda *_: zeros, pipeline_mode=pl.Buffered(1))


def _rms(v, g):
    ms = jnp.mean(v * v, axis=-1, keepdims=True)
    return v * lax.rsqrt(ms + EPS) * g


def _softmax_pv(s, v):
    m = jnp.max(s, axis=-1, keepdims=True)
    p = jnp.exp(s - m)
    l = jnp.sum(p, axis=-1, keepdims=True)
    o = jnp.dot(p.astype(BF16), v, preferred_element_type=F32)
    return o, m, l


def _qk(q, k):
    return lax.dot_general(q, k, (((1,), (1,)), ((), ())), preferred_element_type=F32)


def _in_proj_kernel(x_ref, g_ref, w_ref, hg_ref, cos_ref, sin_ref, pa_ref, pb_ref):
    h = _rms(x_ref[...], g_ref[...]).astype(BF16)
    tm = h.shape[0]
    cos = cos_ref[...]
    sin = sin_ref[...]
    lane = lax.broadcasted_iota(jnp.int32, (tm, HEAD_DIM), 1)
    low_half = (lane % ROPE_AXIS_DIM) < ROPE_PAIRS
    for c0 in range(0, IN_COLS, IN_CHUNK):
        acc = jnp.dot(h, w_ref[:, c0:c0 + IN_CHUNK], preferred_element_type=F32)
        if c0 < ROPE_COLS:
            for hh in range(IN_CHUNK // HEAD_DIM):
                cs = slice(c0 + hh * HEAD_DIM, c0 + (hh + 1) * HEAD_DIM)
                y = _rms(acc[:, hh * HEAD_DIM:(hh + 1) * HEAD_DIM], hg_ref[:, cs])
                partner = jnp.where(low_half, pltpu.roll(y, HEAD_DIM - ROPE_PAIRS, 1),
                                    pltpu.roll(y, ROPE_PAIRS, 1))
                pa_ref[:, cs] = (y * cos + partner * sin).astype(BF16)
        elif c0 < PA_COLS:
            pa_ref[:, c0:c0 + IN_CHUNK] = acc.astype(BF16)
        else:
            pb_ref[:, c0 - PA_COLS:c0 - PA_COLS + IN_CHUNK] = acc


def _in_proj(x2d, gain, w, head_gain, cos_tab, sin_tab, *, seq, tm=512):
    t, d = x2d.shape
    assert t % tm == 0 and seq % tm == 0
    assert ROPE_COLS % IN_CHUNK == 0 and PA_COLS % IN_CHUNK == 0 and IN_COLS % IN_CHUNK == 0
    pos_blocks = seq // tm
    return pl.pallas_call(
        _in_proj_kernel,
        grid=(t // tm,),
        in_specs=[
            pl.BlockSpec((tm, d), lambda i: (i, 0)),
            _resident(gain.shape),
            _resident(w.shape),
            _resident(head_gain.shape),
            pl.BlockSpec((tm, HEAD_DIM), lambda i: (i % pos_blocks, 0)),
            pl.BlockSpec((tm, HEAD_DIM), lambda i: (i % pos_blocks, 0)),
        ],
        out_specs=[pl.BlockSpec((tm, PA_COLS), lambda i: (i, 0)),
                   pl.BlockSpec((tm, PB_COLS), lambda i: (i, 0))],
        out_shape=[jax.ShapeDtypeStruct((t, PA_COLS), BF16),
                   jax.ShapeDtypeStruct((t, PB_COLS), F32)],
        compiler_params=_params("parallel"),
        name="in_proj",
    )(x2d, gain, w, head_gain, cos_tab, sin_tab)


def _norm_matmul_kernel(x_ref, g_ref, w_ref, o_ref, h_scr):
    @pl.when(pl.program_id(1) == 0)
    def _():
        h_scr[...] = _rms(x_ref[...], g_ref[...]).astype(BF16)

    o_ref[...] = jnp.dot(h_scr[...], w_ref[...], preferred_element_type=F32).astype(o_ref.dtype)


def _mem_kv(mem2d, gain, w, *, tm=1024, tn=1024):
    t, d = mem2d.shape
    n = w.shape[1]
    assert t % tm == 0 and n % tn == 0
    return pl.pallas_call(
        _norm_matmul_kernel,
        grid=(t // tm, n // tn),
        in_specs=[
            pl.BlockSpec((tm, d), lambda i, j: (i, 0)),
            pl.BlockSpec((1, d), lambda i, j: (0, 0)),
            pl.BlockSpec((d, tn), lambda i, j: (0, j)),
        ],
        out_specs=pl.BlockSpec((tm, tn), lambda i, j: (i, j)),
        out_shape=jax.ShapeDtypeStruct((t, n), BF16),
        scratch_shapes=[pltpu.VMEM((tm, d), BF16)],
        compiler_params=_params("parallel", "arbitrary"),
        name="mem_kv",
    )(mem2d, gain, w)


def _attn_a_kernel(q_ref, k_ref, v_ref, o_ref, v1_scr):
    @pl.when(pl.program_id(2) == 0)
    def _():
        v1_scr[:, :HEAD_DIM] = v_ref[...]
        v1_scr[:, HEAD_DIM:] = jnp.ones(v_ref.shape, BF16)

    k = k_ref[...]
    v1 = v1_scr[...]
    for g in range(GQA_GROUP):
        cs = slice(g * HEAD_DIM, (g + 1) * HEAD_DIM)
        s = _qk(q_ref[:, cs], k)
        m = jnp.max(s, axis=-1, keepdims=True)
        p = jnp.exp2((s - m) * (ATTN_SCALE * LOG2E)).astype(BF16)
        ol = jnp.dot(p, v1, preferred_element_type=F32)
        o_ref[:, cs] = (ol[:, :HEAD_DIM] / ol[:, HEAD_DIM:]).astype(o_ref.dtype)


def _attn_a(pa, *, tq=512):
    b, s, _ = pa.shape
    qw = GQA_GROUP * HEAD_DIM
    return pl.pallas_call(
        _attn_a_kernel,
        grid=(b, N_KV_A, s // tq),
        in_specs=[
            pl.BlockSpec((None, tq, qw), lambda bi, kv, i: (bi, i, PA_QA // qw + kv)),
            pl.BlockSpec((None, s, HEAD_DIM), lambda bi, kv, i: (bi, 0, PA_KA // HEAD_DIM + kv)),
            pl.BlockSpec((None, s, HEAD_DIM), lambda bi, kv, i: (bi, 0, PA_VA // HEAD_DIM + kv)),
        ],
        out_specs=pl.BlockSpec((None, tq, qw), lambda bi, kv, i: (bi, i, kv)),
        out_shape=jax.ShapeDtypeStruct((b, s, WIDTH_A), BF16),
        scratch_shapes=[pltpu.VMEM((s, 2 * HEAD_DIM), BF16)],
        compiler_params=_params("parallel", "parallel", "arbitrary"),
        name="attn_a",
    )(pa, pa, pa)


B_TQ = 128
B_HALF = 64
assert all(w // (2 * d) == B_HALF for w, d in DIL_PAIRS)


def _band_window(sub_len):
    return min(sub_len, B_TQ + 2 * B_HALF)


def _band_blocks(sub_len):
    w = _band_window(sub_len)
    nblk = sub_len // B_TQ
    out = []
    for qb in range(nblk):
        a = qb * B_TQ
        ks = min(max(a - B_HALF, 0), sub_len - w)
        case = 0 if qb == 0 else (2 if qb == nblk - 1 else 1)
        out.append((a, ks, case))
    return out


def _t5_bucket(rel):
    nb = NUM_BUCKETS // 2
    max_exact = nb // 2
    base = jnp.where(rel > 0, nb, 0)
    n = jnp.abs(rel)
    nf = jnp.maximum(n, 1).astype(F32)
    large = max_exact + (jnp.log(nf / max_exact) / math.log(MAX_DISTANCE / max_exact)
                         * (nb - max_exact)).astype(jnp.int32)
    large = jnp.minimum(large, nb - 1)
    return base + jnp.where(n < max_exact, n, large)


def _band_buckets(sub_len, dil):
    w = _band_window(sub_len)
    offsets = {}
    for a, ks, case in _band_blocks(sub_len):
        assert offsets.setdefault(case, ks - a) == ks - a
    i = np.arange(B_TQ)[:, None]
    j = np.arange(w)[None, :]
    tiles = []
    for case in sorted(offsets):
        rel = jnp.asarray(offsets[case] + j - i, jnp.int32)
        tiles.append(jnp.where(jnp.abs(rel) <= B_HALF, _t5_bucket(rel * dil), -1))
    return jnp.stack(tiles)


def _band_bias_kernel(tab_ref, bucket_ref, o_ref, *, group):
    bucket = bucket_ref[...]
    for h in range(HEADS_PER_DIL):
        acc = jnp.full(bucket.shape, NEG_INF, F32)
        for kb in range(NUM_BUCKETS):
            acc = jnp.where(bucket == kb, tab_ref[kb, group * HEADS_PER_DIL + h], acc)
        o_ref[:, h] = acc


def _band_bias(rel_bias, group, sub_len, dil):
    bucket = _band_buckets(sub_len, dil)
    ncase, tq, w = bucket.shape
    return pl.pallas_call(
        functools.partial(_band_bias_kernel, group=group),
        in_specs=[pl.BlockSpec(memory_space=pltpu.SMEM),
                  pl.BlockSpec(memory_space=pltpu.VMEM)],
        out_specs=pl.BlockSpec(memory_space=pltpu.VMEM),
        out_shape=jax.ShapeDtypeStruct((ncase, HEADS_PER_DIL, tq, w), F32),
        name=f"band_bias_dil{dil}",
    )(rel_bias.astype(F32), bucket)


def _attn_b_kernel(*refs, seq):
    qkv_refs, bias_refs = refs[:3 * N_DIL], refs[3 * N_DIL:4 * N_DIL]
    o_ref, o_scr, lse_scr = refs[4 * N_DIL:]
    for g, (_, dil) in enumerate(DIL_PAIRS):
        q_ref, k_ref, v_ref = qkv_refs[3 * g:3 * g + 3]
        bias_ref = bias_refs[g]
        sub_len = seq // dil
        w = _band_window(sub_len)
        for r in range(dil):
            def rows(start, n):
                return pl.ds(start, n) if dil == 1 else pl.ds(r + dil * start, n, stride=dil)

            for a, ks, case in _band_blocks(sub_len):
                q = q_ref[rows(a, B_TQ), :].astype(BF16)
                k = k_ref[rows(ks, w), :].astype(BF16)
                v = v_ref[rows(ks, w), :].astype(BF16)
                s = _qk(q, k) * ATTN_SCALE + bias_ref[case]
                o, m, l = _softmax_pv(s, v)
                o_scr[g, rows(a, B_TQ), :] = o / l
                lse_scr[g, rows(a, B_TQ), :] = jnp.broadcast_to(m + jnp.log(l), (B_TQ, HEAD_DIM))
    lse = [lse_scr[g] for g in range(N_DIL)]
    mx = functools.reduce(jnp.maximum, lse)
    e = [jnp.exp(x - mx) for x in lse]
    den = functools.reduce(lambda a, b: a + b, e)
    for g in range(N_DIL):
        o_ref[:, g * HEAD_DIM:(g + 1) * HEAD_DIM] = ((e[g] / den) * o_scr[g]).astype(o_ref.dtype)


def _attn_b(pa, pb, biases):
    b, s, _ = pa.shape

    def col(cols0, part, g):
        blk = (cols0 + (3 * g + part) * GROUP_W) // HEAD_DIM
        return pl.BlockSpec((None, s, HEAD_DIM), lambda bi, h: (bi, 0, blk + h))

    in_specs = [col(PA_B0, part, 0) for part in range(3)]
    operands = [pa] * 3
    for g in range(1, N_DIL):
        in_specs += [col(0, part, g - 1) for part in range(3)]
        operands += [pb] * 3
    for bias in biases:
        ncase, _, tq, w = bias.shape
        in_specs.append(pl.BlockSpec((ncase, None, tq, w), lambda bi, h: (0, h, 0, 0)))
    return pl.pallas_call(
        functools.partial(_attn_b_kernel, seq=s),
        grid=(b, HEADS_PER_DIL),
        in_specs=in_specs,
        out_specs=pl.BlockSpec((None, s, N_DIL * HEAD_DIM), lambda bi, h: (bi, 0, h)),
        out_shape=jax.ShapeDtypeStruct((b, s, WIDTH_B), BF16),
        scratch_shapes=[pltpu.VMEM((N_DIL, s, HEAD_DIM), F32), pltpu.VMEM((N_DIL, s, HEAD_DIM), F32)],
        compiler_params=_params("parallel", "parallel"),
        name="attn_b",
    )(*operands, *biases)


def _heads_major(v):
    rest = v.shape[1:]
    v = v.reshape(N_DIL, HEADS_PER_DIL, HEAD_DIM, *rest)
    return jnp.swapaxes(v, 0, 1).reshape(WIDTH_B, *rest)


def _attn_m_kernel(q_ref, kv_ref, o_ref):
    for h in range(N_HEADS_M):
        cs = slice(h * HEAD_DIM, (h + 1) * HEAD_DIM)
        vs = slice(WIDTH_M + h * HEAD_DIM, WIDTH_M + (h + 1) * HEAD_DIM)
        s = _qk(q_ref[:, cs], kv_ref[:, cs]) * ATTN_SCALE
        o, _, l = _softmax_pv(s, kv_ref[:, vs])
        o_ref[:, cs] = (o / l).astype(o_ref.dtype)


def _attn_m(pa, kv, *, tq=1024):
    b, s, _ = pa.shape
    n_mem = kv.shape[1]
    return pl.pallas_call(
        _attn_m_kernel,
        grid=(b, s // tq),
        in_specs=[
            pl.BlockSpec((None, tq, WIDTH_M), lambda bi, i: (bi, i, PA_QM // WIDTH_M)),
            pl.BlockSpec((None, n_mem, 2 * WIDTH_M), lambda bi, i: (bi, 0, 0)),
        ],
        out_specs=pl.BlockSpec((None, tq, WIDTH_M), lambda bi, i: (bi, i, 0)),
        out_shape=jax.ShapeDtypeStruct((b, s, WIDTH_M), BF16),
        compiler_params=_params("parallel", "arbitrary"),
        name="attn_m",
    )(pa, kv)


def _out_proj_kernel(oa_ref, ob_ref, om_ref, x_ref, ga_ref, gb_ref, gm_ref, w_ref, gpost_ref, o_ref):
    mix = jnp.concatenate([_rms(oa_ref[...].astype(F32), ga_ref[...]).astype(BF16),
                           _rms(ob_ref[...].astype(F32), gb_ref[...]).astype(BF16),
                           _rms(om_ref[...].astype(F32), gm_ref[...]).astype(BF16)], axis=-1)
    y = jnp.dot(mix, w_ref[...], preferred_element_type=F32)
    o_ref[...] = x_ref[...] + _rms(y, gpost_ref[...])


def _out_proj(oa, ob, om, x2d, ga, gb, gm, w, gpost, *, tm=512):
    t, d = x2d.shape

    def rows(width):
        return pl.BlockSpec((tm, width), lambda i: (i, 0))

    return pl.pallas_call(
        _out_proj_kernel,
        grid=(t // tm,),
        in_specs=[rows(WIDTH_A), rows(WIDTH_B), rows(WIDTH_M), rows(d),
                  _resident(ga.shape), _resident(gb.shape), _resident(gm.shape),
                  _resident(w.shape), _resident(gpost.shape)],
        out_specs=rows(d),
        out_shape=jax.ShapeDtypeStruct((t, d), F32),
        compiler_params=_params("parallel"),
        name="out_proj",
    )(oa, ob, om, x2d, ga, gb, gm, w, gpost)


def _ffn_kernel(x_ref, gpre_ref, wup_ref, wdn_ref, gpost_ref, o_ref, h_scr):
    j = pl.program_id(1)

    @pl.when(j == 0)
    def _():
        h_scr[...] = _rms(x_ref[...], gpre_ref[...]).astype(BF16)
        o_ref[...] = jnp.zeros_like(o_ref)

    u = jnp.dot(h_scr[...], wup_ref[...], preferred_element_type=F32)
    u = jnp.square(jnp.maximum(u, 0.0)).astype(BF16)
    o_ref[...] += jnp.dot(u, wdn_ref[...], preferred_element_type=F32)

    @pl.when(j == pl.num_programs(1) - 1)
    def _():
        o_ref[...] = x_ref[...] + _rms(o_ref[...], gpost_ref[...])


def _ffn(x2d, gpre, w_up, w_down, gpost, *, tm=1024, tf=512):
    t, d = x2d.shape
    d_ff = w_up.shape[1]
    assert t % tm == 0 and d_ff % tf == 0
    return pl.pallas_call(
        _ffn_kernel,
        grid=(t // tm, d_ff // tf),
        in_specs=[
            pl.BlockSpec((tm, d), lambda i, j: (i, 0)),
            _resident(gpre.shape),
            pl.BlockSpec((d, tf), lambda i, j: (0, j)),
            pl.BlockSpec((tf, d), lambda i, j: (j, 0)),
            _resident(gpost.shape),
        ],
        out_specs=pl.BlockSpec((tm, d), lambda i, j: (i, 0)),
        out_shape=jax.ShapeDtypeStruct((t, d), F32),
        scratch_shapes=[pltpu.VMEM((tm, d), BF16)],
        compiler_params=_params("parallel", "arbitrary"),
        name="ffn",
    )(x2d, gpre, w_up, w_down, gpost)


def _rope_tables(seq_len):
    rows = seq_len // GRID_W
    row = jnp.repeat(jnp.arange(rows), GRID_W).astype(F32)
    col = jnp.tile(jnp.arange(GRID_W), rows).astype(F32)
    inv = ROPE_THETA ** (-(2.0 * jnp.arange(ROPE_PAIRS, dtype=F32)) / ROPE_AXIS_DIM)
    ang_r = row[:, None] * inv
    ang_c = col[:, None] * inv
    cos = jnp.concatenate([jnp.cos(ang_r)] * 2 + [jnp.cos(ang_c)] * 2, axis=-1)
    sin = jnp.concatenate([-jnp.sin(ang_r), jnp.sin(ang_r), -jnp.sin(ang_c), jnp.sin(ang_c)], axis=-1)
    return cos, sin


def _row(v):
    return v.reshape(1, -1).astype(F32)


def _permute_w_in(w):
    splits = [int(c) for c in np.cumsum([WIDTH_A, WIDTH_KV_A, WIDTH_KV_A, WIDTH_B, WIDTH_B, WIDTH_B])]
    qa, ka, va, qb, kb, vb, qm = jnp.split(w, splits, axis=1)

    def grp(m, g):
        return m[:, g * GROUP_W:(g + 1) * GROUP_W]

    parts = [qa, ka, qm, va] + [grp(m, 0) for m in (qb, kb, vb)]
    for g in range(1, N_DIL):
        parts += [grp(m, g) for m in (qb, kb, vb)]
    return jnp.concatenate(parts, axis=1)


def _encoder_layer(x, mem, p, biases):
    b, s, d = x.shape
    x2d = x.reshape(b * s, d)
    cos_tab, sin_tab = _rope_tables(s)
    pa, pb = _in_proj(x2d, p["pre_mix_norm"], p["w_in"], p["head_gain"], cos_tab, sin_tab, seq=s)
    pa = pa.reshape(b, s, PA_COLS)
    pb = pb.reshape(b, s, PB_COLS)
    kv = _mem_kv(mem.reshape(-1, d), p["mem_norm"], p["w_mem_kv"]).reshape(b, mem.shape[1], -1)

    oa = _attn_a(pa).reshape(b * s, WIDTH_A)
    ob = _attn_b(pa, pb, biases).reshape(b * s, WIDTH_B)
    om = _attn_m(pa, kv).reshape(b * s, WIDTH_M)

    x1 = _out_proj(oa, ob, om, x2d, p["out_norm_a"], p["out_norm_b"], p["out_norm_m"],
                   p["w_out"], p["post_mix_norm"])
    y = _ffn(x1, p["pre_ffn_norm"], p["w_up"], p["w_down"], p["post_ffn_norm"])
    return y.reshape(b, s, d)


def kernel(x_prompt, x_sample, mem_prompt, mem_sample, rel_bias, pre_mix_norm, w_in, q_norm_a, k_norm_a, mem_norm, w_mem_kv, out_norm_a, out_norm_b, out_norm_m, w_out, post_mix_norm, pre_ffn_norm, w_up, w_down, post_ffn_norm):
    depth = w_in.shape[0]
    assert x_prompt.shape[1] == x_sample.shape[1]
    seq = x_prompt.shape[1]
    biases = [_band_bias(rel_bias, g, seq // dil, dil) for g, (_, dil) in enumerate(DIL_PAIRS)]
    layers = []
    for l in range(depth):
        w_o = w_out[l]
        w_o = jnp.concatenate([w_o[:WIDTH_A], _heads_major(w_o[WIDTH_A:WIDTH_A + WIDTH_B]),
                               w_o[WIDTH_A + WIDTH_B:]], axis=0)
        layers.append({
            "pre_mix_norm": _row(pre_mix_norm[l]),
            "w_in": _permute_w_in(w_in[l]).astype(BF16),
            "head_gain": _row(jnp.concatenate([jnp.tile(q_norm_a[l], N_HEADS_A),
                                               jnp.tile(k_norm_a[l], N_KV_A)])),
            "mem_norm": _row(mem_norm[l]),
            "w_mem_kv": w_mem_kv[l].astype(BF16),
            "out_norm_a": _row(out_norm_a[l]),
            "out_norm_b": _row(_heads_major(out_norm_b[l])),
            "out_norm_m": _row(out_norm_m[l]),
            "w_out": w_o.astype(BF16),
            "post_mix_norm": _row(post_mix_norm[l]),
            "pre_ffn_norm": _row(pre_ffn_norm[l]),
            "w_up": w_up[l].astype(BF16),
            "w_down": w_down[l].astype(BF16),
            "post_ffn_norm": _row(post_ffn_norm[l]),
        })

    def trunk(x, mem):
        for p in layers:
            x = _encoder_layer(x, mem, p, biases)
        return x

    return (trunk(x_prompt, mem_prompt), trunk(x_sample, mem_sample))
```

```python
import functools
import math

import jax
import jax.numpy as jnp
import numpy as np
from jax import lax
from jax.experimental import pallas as pl
from jax.experimental.pallas import tpu as pltpu

F32 = jnp.float32
BF16 = jnp.bfloat16

HEAD_DIM = 128
N_HEADS_A = 6
N_KV_A = 2
GQA_GROUP = N_HEADS_A // N_KV_A
DIL_PAIRS = ((128, 1), (512, 4), (2048, 16))
N_DIL = len(DIL_PAIRS)
HEADS_PER_DIL = 2
N_HEADS_B = N_DIL * HEADS_PER_DIL
N_HEADS_M = 4
WIDTH_A = N_HEADS_A * HEAD_DIM
WIDTH_KV_A = N_KV_A * HEAD_DIM
WIDTH_B = N_HEADS_B * HEAD_DIM
WIDTH_M = N_HEADS_M * HEAD_DIM
GROUP_W = HEADS_PER_DIL * HEAD_DIM
GRID_W = 64
ROPE_THETA = 10000.0
ROPE_AXIS_DIM = HEAD_DIM // 2
ROPE_PAIRS = ROPE_AXIS_DIM // 2
NUM_BUCKETS = 32
MAX_DISTANCE = 1024
EPS = 1e-6
NEG_INF = -1e30
ATTN_SCALE = HEAD_DIM ** -0.5
LOG2E = math.log2(math.e)

PA_QA = 0
PA_KA = PA_QA + WIDTH_A
ROPE_COLS = PA_KA + WIDTH_KV_A
PA_QM = ROPE_COLS
PA_VA = PA_QM + WIDTH_M
PA_B0 = PA_VA + WIDTH_KV_A
PA_COLS = PA_B0 + 3 * GROUP_W
PB_COLS = (N_DIL - 1) * 3 * GROUP_W
assert PA_QA % (GQA_GROUP * HEAD_DIM) == 0 and PA_QM % WIDTH_M == 0
IN_COLS = PA_COLS + PB_COLS
IN_CHUNK = 512

V7X_VMEM_BYTES = 64 * 1024 * 1024
VMEM_LIMIT_BYTES = V7X_VMEM_BYTES - 4 * 1024 * 1024


def _params(*semantics):
    return pltpu.CompilerParams(dimension_semantics=semantics,
                                vmem_limit_bytes=VMEM_LIMIT_BYTES)


def _resident(shape):
    zeros = (0,) * len(shape)
    return pl.BlockSpec(shape, lambda *_: zeros, pipeline_mode=pl.Buffered(1))


def _rms(v, g):
    ms = jnp.mean(v * v, axis=-1, keepdims=True)
    return v * lax.rsqrt(ms + EPS) * g


def _row_halves(rows):
    return (slice(0, rows // 2), slice(rows // 2, rows))


def _softmax_pv(s, v):
    m = jnp.max(s, axis=-1, keepdims=True)
    p = jnp.exp(s - m)
    l = jnp.sum(p, axis=-1, keepdims=True)
    o = jnp.dot(p.astype(BF16), v, preferred_element_type=F32)
    return o, m, l


def _qk(q, k):
    return lax.dot_general(q, k, (((1,), (1,)), ((), ())), preferred_element_type=F32)


def _in_proj_kernel(x_ref, g_ref, w_ref, hg_ref, cos_ref, sin_ref, pa_ref, pb_ref):
    h = _rms(x_ref[...], g_ref[...]).astype(BF16)
    tm = h.shape[0]
    cos = cos_ref[...]
    sin = sin_ref[...]
    lane = lax.broadcasted_iota(jnp.int32, (tm, HEAD_DIM), 1)
    low_half = (lane % ROPE_AXIS_DIM) < ROPE_PAIRS
    for c0 in range(0, IN_COLS, IN_CHUNK):
        acc = jnp.dot(h, w_ref[:, c0:c0 + IN_CHUNK], preferred_element_type=F32)
        if c0 < ROPE_COLS:
            for hh in range(IN_CHUNK // HEAD_DIM):
                cs = slice(c0 + hh * HEAD_DIM, c0 + (hh + 1) * HEAD_DIM)
                y = _rms(acc[:, hh * HEAD_DIM:(hh + 1) * HEAD_DIM], hg_ref[:, cs])
                partner = jnp.where(low_half, pltpu.roll(y, HEAD_DIM - ROPE_PAIRS, 1),
                                    pltpu.roll(y, ROPE_PAIRS, 1))
                pa_ref[:, cs] = (y * cos + partner * sin).astype(BF16)
        elif c0 < PA_COLS:
            pa_ref[:, c0:c0 + IN_CHUNK] = acc.astype(BF16)
        else:
            pb_ref[:, c0 - PA_COLS:c0 - PA_COLS + IN_CHUNK] = acc


def _in_proj(x2d, gain, w, head_gain, cos_tab, sin_tab, *, seq, tm=512):
    t, d = x2d.shape
    assert t % tm == 0 and seq % tm == 0
    assert ROPE_COLS % IN_CHUNK == 0 and PA_COLS % IN_CHUNK == 0 and IN_COLS % IN_CHUNK == 0
    pos_blocks = seq // tm
    return pl.pallas_call(
        _in_proj_kernel,
        grid=(t // tm,),
        in_specs=[
            pl.BlockSpec((tm, d), lambda i: (i, 0)),
            _resident(gain.shape),
            _resident(w.shape),
            _resident(head_gain.shape),
            pl.BlockSpec((tm, HEAD_DIM), lambda i: (i % pos_blocks, 0)),
            pl.BlockSpec((tm, HEAD_DIM), lambda i: (i % pos_blocks, 0)),
        ],
        out_specs=[pl.BlockSpec((tm, PA_COLS), lambda i: (i, 0)),
                   pl.BlockSpec((tm, PB_COLS), lambda i: (i, 0))],
        out_shape=[jax.ShapeDtypeStruct((t, PA_COLS), BF16),
                   jax.ShapeDtypeStruct((t, PB_COLS), F32)],
        compiler_params=_params("parallel"),
        name="in_proj",
    )(x2d, gain, w, head_gain, cos_tab, sin_tab)


def _norm_matmul_kernel(x_ref, g_ref, w_ref, o_ref, h_scr):
    @pl.when(pl.program_id(1) == 0)
    def _():
        h_scr[...] = _rms(x_ref[...], g_ref[...]).astype(BF16)

    o_ref[...] = jnp.dot(h_scr[...], w_ref[...], preferred_element_type=F32).astype(o_ref.dtype)


def _mem_kv(mem2d, gain, w, *, tm=1024, tn=1024):
    t, d = mem2d.shape
    n = w.shape[1]
    assert t % tm == 0 and n % tn == 0
    return pl.pallas_call(
        _norm_matmul_kernel,
        grid=(t // tm, n // tn),
        in_specs=[
            pl.BlockSpec((tm, d), lambda i, j: (i, 0)),
            pl.BlockSpec((1, d), lambda i, j: (0, 0)),
            pl.BlockSpec((d, tn), lambda i, j: (0, j)),
        ],
        out_specs=pl.BlockSpec((tm, tn), lambda i, j: (i, j)),
        out_shape=jax.ShapeDtypeStruct((t, n), BF16),
        scratch_shapes=[pltpu.VMEM((tm, d), BF16)],
        compiler_params=_params("parallel", "arbitrary"),
        name="mem_kv",
    )(mem2d, gain, w)


def _attn_a_kernel(q_ref, k_ref, v_ref, o_ref, v1_scr):
    @pl.when(pl.program_id(1) == 0)
    def _():
        for kv in range(N_KV_A):
            v1_scr[kv, :, :HEAD_DIM] = v_ref[:, kv * HEAD_DIM:(kv + 1) * HEAD_DIM]
            v1_scr[kv, :, HEAD_DIM:] = jnp.ones((v_ref.shape[0], HEAD_DIM), BF16)

    for kv in range(N_KV_A):
        k = k_ref[:, kv * HEAD_DIM:(kv + 1) * HEAD_DIM]
        v1 = v1_scr[kv]
        for g in range(GQA_GROUP):
            head = kv * GQA_GROUP + g
            cs = slice(head * HEAD_DIM, (head + 1) * HEAD_DIM)
            s = _qk(q_ref[:, cs], k)
            m = jnp.max(s, axis=-1, keepdims=True)
            p = jnp.exp2((s - m) * (ATTN_SCALE * LOG2E)).astype(BF16)
            ol = jnp.dot(p, v1, preferred_element_type=F32)
            o_ref[:, cs] = (ol[:, :HEAD_DIM] / ol[:, HEAD_DIM:]).astype(o_ref.dtype)


def _attn_a(pa, *, tq=1024):
    b, s, _ = pa.shape
    assert s % tq == 0 and PA_KA % WIDTH_KV_A == 0 and PA_VA % WIDTH_KV_A == 0
    return pl.pallas_call(
        _attn_a_kernel,
        grid=(b, s // tq),
        in_specs=[
            pl.BlockSpec((None, tq, WIDTH_A), lambda bi, i: (bi, i, PA_QA // WIDTH_A)),
            pl.BlockSpec((None, s, WIDTH_KV_A), lambda bi, i: (bi, 0, PA_KA // WIDTH_KV_A)),
            pl.BlockSpec((None, s, WIDTH_KV_A), lambda bi, i: (bi, 0, PA_VA // WIDTH_KV_A)),
        ],
        out_specs=pl.BlockSpec((None, tq, WIDTH_A), lambda bi, i: (bi, i, 0)),
        out_shape=jax.ShapeDtypeStruct((b, s, WIDTH_A), BF16),
        scratch_shapes=[pltpu.VMEM((N_KV_A, s, 2 * HEAD_DIM), BF16)],
        compiler_params=_params("parallel", "arbitrary"),
        name="attn_a",
    )(pa, pa, pa)


B_TQ = 128
B_HALF = 64
assert all(w // (2 * d) == B_HALF for w, d in DIL_PAIRS)


def _band_window(sub_len):
    return min(sub_len, B_TQ + 2 * B_HALF)


def _band_blocks(sub_len):
    w = _band_window(sub_len)
    nblk = sub_len // B_TQ
    out = []
    for qb in range(nblk):
        a = qb * B_TQ
        ks = min(max(a - B_HALF, 0), sub_len - w)
        case = 0 if qb == 0 else (2 if qb == nblk - 1 else 1)
        out.append((a, ks, case))
    return out


def _t5_bucket(rel):
    nb = NUM_BUCKETS // 2
    max_exact = nb // 2
    base = jnp.where(rel > 0, nb, 0)
    n = jnp.abs(rel)
    nf = jnp.maximum(n, 1).astype(F32)
    large = max_exact + (jnp.log(nf / max_exact) / math.log(MAX_DISTANCE / max_exact)
                         * (nb - max_exact)).astype(jnp.int32)
    large = jnp.minimum(large, nb - 1)
    return base + jnp.where(n < max_exact, n, large)


def _band_buckets(sub_len, dil):
    w = _band_window(sub_len)
    offsets = {}
    for a, ks, case in _band_blocks(sub_len):
        assert offsets.setdefault(case, ks - a) == ks - a
    i = np.arange(B_TQ)[:, None]
    j = np.arange(w)[None, :]
    tiles = []
    for case in sorted(offsets):
        rel = jnp.asarray(offsets[case] + j - i, jnp.int32)
        tiles.append(jnp.where(jnp.abs(rel) <= B_HALF, _t5_bucket(rel * dil), -1))
    return jnp.stack(tiles)


def _band_bias_kernel(tab_ref, bucket_ref, o_ref, *, group):
    bucket = bucket_ref[...]
    for h in range(HEADS_PER_DIL):
        acc = jnp.full(bucket.shape, NEG_INF, F32)
        for kb in range(NUM_BUCKETS):
            acc = jnp.where(bucket == kb, tab_ref[kb, group * HEADS_PER_DIL + h], acc)
        o_ref[:, h] = acc


def _band_bias(rel_bias, group, sub_len, dil):
    bucket = _band_buckets(sub_len, dil)
    ncase, tq, w = bucket.shape
    return pl.pallas_call(
        functools.partial(_band_bias_kernel, group=group),
        in_specs=[pl.BlockSpec(memory_space=pltpu.SMEM),
                  pl.BlockSpec(memory_space=pltpu.VMEM)],
        out_specs=pl.BlockSpec(memory_space=pltpu.VMEM),
        out_shape=jax.ShapeDtypeStruct((ncase, HEADS_PER_DIL, tq, w), F32),
        name=f"band_bias_dil{dil}",
    )(rel_bias.astype(F32), bucket)


def _attn_b_kernel(*refs, seq):
    qkv_refs, bias_refs = refs[:3 * N_DIL], refs[3 * N_DIL:4 * N_DIL]
    o_ref, o_scr, lse_scr = refs[4 * N_DIL:]
    for g, (_, dil) in enumerate(DIL_PAIRS):
        q_ref, k_ref, v_ref = qkv_refs[3 * g:3 * g + 3]
        bias_ref = bias_refs[g]
        sub_len = seq // dil
        w = _band_window(sub_len)
        for r in range(dil):
            def rows(start, n):
                return pl.ds(start, n) if dil == 1 else pl.ds(r + dil * start, n, stride=dil)

            for a, ks, case in _band_blocks(sub_len):
                q = q_ref[rows(a, B_TQ), :].astype(BF16)
                k = k_ref[rows(ks, w), :].astype(BF16)
                v = v_ref[rows(ks, w), :].astype(BF16)
                s = _qk(q, k) * ATTN_SCALE + bias_ref[case]
                o, m, l = _softmax_pv(s, v)
                o_scr[g, rows(a, B_TQ), :] = o / l
                lse_scr[g, rows(a, B_TQ), :] = jnp.broadcast_to(m + jnp.log(l), (B_TQ, HEAD_DIM))
    lse = [lse_scr[g] for g in range(N_DIL)]
    mx = functools.reduce(jnp.maximum, lse)
    e = [jnp.exp(x - mx) for x in lse]
    den = functools.reduce(lambda a, b: a + b, e)
    for g in range(N_DIL):
        o_ref[:, g * HEAD_DIM:(g + 1) * HEAD_DIM] = ((e[g] / den) * o_scr[g]).astype(o_ref.dtype)


def _attn_b(pa, pb, biases):
    b, s, _ = pa.shape

    def col(cols0, part, g):
        blk = (cols0 + (3 * g + part) * GROUP_W) // HEAD_DIM
        return pl.BlockSpec((None, s, HEAD_DIM), lambda bi, h: (bi, 0, blk + h))

    in_specs = [col(PA_B0, part, 0) for part in range(3)]
    operands = [pa] * 3
    for g in range(1, N_DIL):
        in_specs += [col(0, part, g - 1) for part in range(3)]
        operands += [pb] * 3
    for bias in biases:
        ncase, _, tq, w = bias.shape
        in_specs.append(pl.BlockSpec((ncase, None, tq, w), lambda bi, h: (0, h, 0, 0)))
    return pl.pallas_call(
        functools.partial(_attn_b_kernel, seq=s),
        grid=(b, HEADS_PER_DIL),
        in_specs=in_specs,
        out_specs=pl.BlockSpec((None, s, N_DIL * HEAD_DIM), lambda bi, h: (bi, 0, h)),
        out_shape=jax.ShapeDtypeStruct((b, s, WIDTH_B), BF16),
        scratch_shapes=[pltpu.VMEM((N_DIL, s, HEAD_DIM), F32), pltpu.VMEM((N_DIL, s, HEAD_DIM), F32)],
        compiler_params=_params("parallel", "parallel"),
        name="attn_b",
    )(*operands, *biases)


def _heads_major(v):
    rest = v.shape[1:]
    v = v.reshape(N_DIL, HEADS_PER_DIL, HEAD_DIM, *rest)
    return jnp.swapaxes(v, 0, 1).reshape(WIDTH_B, *rest)


def _attn_m_kernel(q_ref, kv_ref, o_ref):
    for h in range(N_HEADS_M):
        cs = slice(h * HEAD_DIM, (h + 1) * HEAD_DIM)
        vs = slice(WIDTH_M + h * HEAD_DIM, WIDTH_M + (h + 1) * HEAD_DIM)
        s = _qk(q_ref[:, cs], kv_ref[:, cs]) * ATTN_SCALE
        o, _, l = _softmax_pv(s, kv_ref[:, vs])
        o_ref[:, cs] = (o / l).astype(o_ref.dtype)


def _attn_m(pa, kv, *, tq=1024):
    b, s, _ = pa.shape
    n_mem = kv.shape[1]
    return pl.pallas_call(
        _attn_m_kernel,
        grid=(b, s // tq),
        in_specs=[
            pl.BlockSpec((None, tq, WIDTH_M), lambda bi, i: (bi, i, PA_QM // WIDTH_M)),
            pl.BlockSpec((None, n_mem, 2 * WIDTH_M), lambda bi, i: (bi, 0, 0)),
        ],
        out_specs=pl.BlockSpec((None, tq, WIDTH_M), lambda bi, i: (bi, i, 0)),
        out_shape=jax.ShapeDtypeStruct((b, s, WIDTH_M), BF16),
        compiler_params=_params("parallel", "arbitrary"),
        name="attn_m",
    )(pa, kv)


def _out_proj_kernel(oa_ref, ob_ref, om_ref, x_ref, ga_ref, gb_ref, gm_ref, w_ref, gpost_ref, o_ref):
    for rs in _row_halves(x_ref.shape[0]):
        mix = jnp.concatenate([_rms(oa_ref[rs, :].astype(F32), ga_ref[...]).astype(BF16),
                               _rms(ob_ref[rs, :].astype(F32), gb_ref[...]).astype(BF16),
                               _rms(om_ref[rs, :].astype(F32), gm_ref[...]).astype(BF16)], axis=-1)
        y = jnp.dot(mix, w_ref[...], preferred_element_type=F32)
        o_ref[rs, :] = x_ref[rs, :] + _rms(y, gpost_ref[...])


def _out_proj(oa, ob, om, x2d, ga, gb, gm, w, gpost, *, tm=1024):
    t, d = x2d.shape
    assert t % tm == 0

    def rows(width):
        return pl.BlockSpec((tm, width), lambda i: (i, 0))

    return pl.pallas_call(
        _out_proj_kernel,
        grid=(t // tm,),
        in_specs=[rows(WIDTH_A), rows(WIDTH_B), rows(WIDTH_M), rows(d),
                  _resident(ga.shape), _resident(gb.shape), _resident(gm.shape),
                  _resident(w.shape), _resident(gpost.shape)],
        out_specs=rows(d),
        out_shape=jax.ShapeDtypeStruct((t, d), F32),
        compiler_params=_params("parallel"),
        name="out_proj",
    )(oa, ob, om, x2d, ga, gb, gm, w, gpost)


def _ffn_kernel(x_ref, gpre_ref, wup_ref, wdn_ref, gpost_ref, o_ref, h_scr):
    j = pl.program_id(1)
    last = pl.num_programs(1) - 1
    tm = x_ref.shape[0]

    def step(first, final):
        for rs in (_row_halves(tm) if first or final else (slice(0, tm),)):
            if first:
                h = _rms(x_ref[rs, :], gpre_ref[...]).astype(BF16)
                h_scr[rs, :] = h
            else:
                h = h_scr[rs, :]
            u = jnp.dot(h, wup_ref[...], preferred_element_type=F32)
            u = jnp.square(jnp.maximum(u, 0.0)).astype(BF16)
            acc = jnp.dot(u, wdn_ref[...], preferred_element_type=F32)
            if not first:
                acc = o_ref[rs, :] + acc
            if final:
                acc = x_ref[rs, :] + _rms(acc, gpost_ref[...])
            o_ref[rs, :] = acc

    pl.when(j == 0)(functools.partial(step, True, False))
    pl.when((j > 0) & (j < last))(functools.partial(step, False, False))
    pl.when(j == last)(functools.partial(step, False, True))


def _ffn(x2d, gpre, w_up, w_down, gpost, *, tm=1024, tf=512):
    t, d = x2d.shape
    d_ff = w_up.shape[1]
    assert t % tm == 0 and d_ff % tf == 0 and d_ff // tf >= 2
    return pl.pallas_call(
        _ffn_kernel,
        grid=(t // tm, d_ff // tf),
        in_specs=[
            pl.BlockSpec((tm, d), lambda i, j: (i, 0)),
            _resident(gpre.shape),
            pl.BlockSpec((d, tf), lambda i, j: (0, j)),
            pl.BlockSpec((tf, d), lambda i, j: (j, 0)),
            _resident(gpost.shape),
        ],
        out_specs=pl.BlockSpec((tm, d), lambda i, j: (i, 0)),
        out_shape=jax.ShapeDtypeStruct((t, d), F32),
        scratch_shapes=[pltpu.VMEM((tm, d), BF16)],
        compiler_params=_params("parallel", "arbitrary"),
        name="ffn",
    )(x2d, gpre, w_up, w_down, gpost)


def _rope_tables(seq_len):
    rows = seq_len // GRID_W
    row = jnp.repeat(jnp.arange(rows), GRID_W).astype(F32)
    col = jnp.tile(jnp.arange(GRID_W), rows).astype(F32)
    inv = ROPE_THETA ** (-(2.0 * jnp.arange(ROPE_PAIRS, dtype=F32)) / ROPE_AXIS_DIM)
    ang_r = row[:, None] * inv
    ang_c = col[:, None] * inv
    cos = jnp.concatenate([jnp.cos(ang_r)] * 2 + [jnp.cos(ang_c)] * 2, axis=-1)
    sin = jnp.concatenate([-jnp.sin(ang_r), jnp.sin(ang_r), -jnp.sin(ang_c), jnp.sin(ang_c)], axis=-1)
    return cos, sin


def _row(v):
    return v.reshape(1, -1).astype(F32)


def _permute_w_in(w):
    splits = [int(c) for c in np.cumsum([WIDTH_A, WIDTH_KV_A, WIDTH_KV_A, WIDTH_B, WIDTH_B, WIDTH_B])]
    qa, ka, va, qb, kb, vb, qm = jnp.split(w, splits, axis=1)

    def grp(m, g):
        return m[:, g * GROUP_W:(g + 1) * GROUP_W]

    parts = [qa, ka, qm, va] + [grp(m, 0) for m in (qb, kb, vb)]
    for g in range(1, N_DIL):
        parts += [grp(m, g) for m in (qb, kb, vb)]
    return jnp.concatenate(parts, axis=1)


def _encoder_layer(x, mem, p, biases):
    b, s, d = x.shape
    x2d = x.reshape(b * s, d)
    cos_tab, sin_tab = _rope_tables(s)
    pa, pb = _in_proj(x2d, p["pre_mix_norm"], p["w_in"], p["head_gain"], cos_tab, sin_tab, seq=s)
    pa = pa.reshape(b, s, PA_COLS)
    pb = pb.reshape(b, s, PB_COLS)
    kv = _mem_kv(mem.reshape(-1, d), p["mem_norm"], p["w_mem_kv"]).reshape(b, mem.shape[1], -1)

    oa = _attn_a(pa).reshape(b * s, WIDTH_A)
    ob = _attn_b(pa, pb, biases).reshape(b * s, WIDTH_B)
    om = _attn_m(pa, kv).reshape(b * s, WIDTH_M)

    x1 = _out_proj(oa, ob, om, x2d, p["out_norm_a"], p["out_norm_b"], p["out_norm_m"],
                   p["w_out"], p["post_mix_norm"])
    y = _ffn(x1, p["pre_ffn_norm"], p["w_up"], p["w_down"], p["post_ffn_norm"])
    return y.reshape(b, s, d)


def kernel(x_prompt, x_sample, mem_prompt, mem_sample, rel_bias, pre_mix_norm, w_in, q_norm_a, k_norm_a, mem_norm, w_mem_kv, out_norm_a, out_norm_b, out_norm_m, w_out, post_mix_norm, pre_ffn_norm, w_up, w_down, post_ffn_norm):
    depth = w_in.shape[0]
    assert x_prompt.shape[1] == x_sample.shape[1]
    seq = x_prompt.shape[1]
    biases = [_band_bias(rel_bias, g, seq // dil, dil) for g, (_, dil) in enumerate(DIL_PAIRS)]
    layers = []
    for l in range(depth):
        w_o = w_out[l]
        w_o = jnp.concatenate([w_o[:WIDTH_A], _heads_major(w_o[WIDTH_A:WIDTH_A + WIDTH_B]),
                               w_o[WIDTH_A + WIDTH_B:]], axis=0)
        layers.append({
            "pre_mix_norm": _row(pre_mix_norm[l]),
            "w_in": _permute_w_in(w_in[l]).astype(BF16),
            "head_gain": _row(jnp.concatenate([jnp.tile(q_norm_a[l], N_HEADS_A),
                                               jnp.tile(k_norm_a[l], N_KV_A)])),
            "mem_norm": _row(mem_norm[l]),
            "w_mem_kv": w_mem_kv[l].astype(BF16),
            "out_norm_a": _row(out_norm_a[l]),
            "out_norm_b": _row(_heads_major(out_norm_b[l])),
            "out_norm_m": _row(out_norm_m[l]),
            "w_out": w_o.astype(BF16),
            "post_mix_norm": _row(post_mix_norm[l]),
            "pre_ffn_norm": _row(pre_ffn_norm[l]),
            "w_up": w_up[l].astype(BF16),
            "w_down": w_down[l].astype(BF16),
            "post_ffn_norm": _row(post_ffn_norm[l]),
        })

    def trunk(x, mem):
        for p in layers:
            x = _encoder_layer(x, mem, p, biases)
        return x

    return (trunk(x_prompt, mem_prompt), trunk(x_sample, mem_sample))
```

```python
import functools
import math

import jax
import jax.numpy as jnp
import numpy as np
from jax import lax
from jax.experimental import pallas as pl
from jax.experimental.pallas import tpu as pltpu

F32 = jnp.float32
BF16 = jnp.bfloat16

HEAD_DIM = 128
N_HEADS_A = 6
N_KV_A = 2
GQA_GROUP = N_HEADS_A // N_KV_A
DIL_PAIRS = ((128, 1), (512, 4), (2048, 16))
N_DIL = len(DIL_PAIRS)
HEADS_PER_DIL = 2
N_HEADS_B = N_DIL * HEADS_PER_DIL
N_HEADS_M = 4
WIDTH_A = N_HEADS_A * HEAD_DIM
WIDTH_KV_A = N_KV_A * HEAD_DIM
WIDTH_B = N_HEADS_B * HEAD_DIM
WIDTH_M = N_HEADS_M * HEAD_DIM
GROUP_W = HEADS_PER_DIL * HEAD_DIM
GRID_W = 64
ROPE_THETA = 10000.0
ROPE_AXIS_DIM = HEAD_DIM // 2
ROPE_PAIRS = ROPE_AXIS_DIM // 2
NUM_BUCKETS = 32
MAX_DISTANCE = 1024
EPS = 1e-6
NEG_INF = -1e30
ATTN_SCALE = HEAD_DIM ** -0.5
LOG2E = math.log2(math.e)

PA_QA = 0
PA_KA = PA_QA + WIDTH_A
ROPE_COLS = PA_KA + WIDTH_KV_A
PA_QM = ROPE_COLS
PA_VA = PA_QM + WIDTH_M
PA_B0 = PA_VA + WIDTH_KV_A
PA_COLS = PA_B0 + 3 * GROUP_W
PB_COLS = (N_DIL - 1) * 3 * GROUP_W
assert PA_QA % (GQA_GROUP * HEAD_DIM) == 0 and PA_QM % WIDTH_M == 0
IN_COLS = PA_COLS + PB_COLS
IN_CHUNK = 512
FFN_CHUNK = 1024

V7X_VMEM_BYTES = 64 * 1024 * 1024
VMEM_LIMIT_BYTES = V7X_VMEM_BYTES - 4 * 1024 * 1024


def _params(*semantics):
    return pltpu.CompilerParams(dimension_semantics=semantics,
                                vmem_limit_bytes=VMEM_LIMIT_BYTES)


def _resident(shape):
    zeros = (0,) * len(shape)
    return pl.BlockSpec(shape, lambda *_: zeros, pipeline_mode=pl.Buffered(1))


def _rms(v, g):
    ms = jnp.mean(v * v, axis=-1, keepdims=True)
    return v * lax.rsqrt(ms + EPS) * g


def _row_halves(rows):
    return (slice(0, rows // 2), slice(rows // 2, rows))


def _softmax_pv(s, v):
    m = jnp.max(s, axis=-1, keepdims=True)
    p = jnp.exp(s - m)
    l = jnp.sum(p, axis=-1, keepdims=True)
    o = jnp.dot(p.astype(BF16), v, preferred_element_type=F32)
    return o, m, l


def _qk(q, k):
    return lax.dot_general(q, k, (((1,), (1,)), ((), ())), preferred_element_type=F32)


def _in_proj_kernel(x_ref, g_ref, w_ref, hg_ref, cos_ref, sin_ref, pa_ref, pb_ref):
    h = _rms(x_ref[...], g_ref[...]).astype(BF16)
    tm = h.shape[0]
    cos = cos_ref[...]
    sin = sin_ref[...]
    lane = lax.broadcasted_iota(jnp.int32, (tm, HEAD_DIM), 1)
    low_half = (lane % ROPE_AXIS_DIM) < ROPE_PAIRS
    for c0 in range(0, IN_COLS, IN_CHUNK):
        acc = jnp.dot(h, w_ref[:, c0:c0 + IN_CHUNK], preferred_element_type=F32)
        if c0 < ROPE_COLS:
            for hh in range(IN_CHUNK // HEAD_DIM):
                cs = slice(c0 + hh * HEAD_DIM, c0 + (hh + 1) * HEAD_DIM)
                y = _rms(acc[:, hh * HEAD_DIM:(hh + 1) * HEAD_DIM], hg_ref[:, cs])
                partner = jnp.where(low_half, pltpu.roll(y, HEAD_DIM - ROPE_PAIRS, 1),
                                    pltpu.roll(y, ROPE_PAIRS, 1))
                pa_ref[:, cs] = (y * cos + partner * sin).astype(BF16)
        elif c0 < PA_COLS:
            pa_ref[:, c0:c0 + IN_CHUNK] = acc.astype(BF16)
        else:
            pb_ref[:, c0 - PA_COLS:c0 - PA_COLS + IN_CHUNK] = acc


def _in_proj(x2d, gain, w, head_gain, cos_tab, sin_tab, *, seq, tm=512):
    t, d = x2d.shape
    assert t % tm == 0 and seq % tm == 0
    assert ROPE_COLS % IN_CHUNK == 0 and PA_COLS % IN_CHUNK == 0 and IN_COLS % IN_CHUNK == 0
    pos_blocks = seq // tm
    return pl.pallas_call(
        _in_proj_kernel,
        grid=(t // tm,),
        in_specs=[
            pl.BlockSpec((tm, d), lambda i: (i, 0)),
            _resident(gain.shape),
            _resident(w.shape),
            _resident(head_gain.shape),
            pl.BlockSpec((tm, HEAD_DIM), lambda i: (i % pos_blocks, 0)),
            pl.BlockSpec((tm, HEAD_DIM), lambda i: (i % pos_blocks, 0)),
        ],
        out_specs=[pl.BlockSpec((tm, PA_COLS), lambda i: (i, 0)),
                   pl.BlockSpec((tm, PB_COLS), lambda i: (i, 0))],
        out_shape=[jax.ShapeDtypeStruct((t, PA_COLS), BF16),
                   jax.ShapeDtypeStruct((t, PB_COLS), F32)],
        compiler_params=_params("parallel"),
        name="in_proj",
    )(x2d, gain, w, head_gain, cos_tab, sin_tab)


def _norm_matmul_kernel(x_ref, g_ref, w_ref, o_ref, h_scr):
    @pl.when(pl.program_id(1) == 0)
    def _():
        h_scr[...] = _rms(x_ref[...], g_ref[...]).astype(BF16)

    o_ref[...] = jnp.dot(h_scr[...], w_ref[...], preferred_element_type=F32).astype(o_ref.dtype)


def _mem_kv(mem2d, gain, w, *, tm=1024, tn=1024):
    t, d = mem2d.shape
    n = w.shape[1]
    assert t % tm == 0 and n % tn == 0
    return pl.pallas_call(
        _norm_matmul_kernel,
        grid=(t // tm, n // tn),
        in_specs=[
            pl.BlockSpec((tm, d), lambda i, j: (i, 0)),
            pl.BlockSpec((1, d), lambda i, j: (0, 0)),
            pl.BlockSpec((d, tn), lambda i, j: (0, j)),
        ],
        out_specs=pl.BlockSpec((tm, tn), lambda i, j: (i, j)),
        out_shape=jax.ShapeDtypeStruct((t, n), BF16),
        scratch_shapes=[pltpu.VMEM((tm, d), BF16)],
        compiler_params=_params("parallel", "arbitrary"),
        name="mem_kv",
    )(mem2d, gain, w)


def _attn_a_kernel(q_ref, k_ref, v_ref, o_ref, v1_scr):
    @pl.when(pl.program_id(1) == 0)
    def _():
        for kv in range(N_KV_A):
            v1_scr[kv, :, :HEAD_DIM] = v_ref[:, kv * HEAD_DIM:(kv + 1) * HEAD_DIM]
            v1_scr[kv, :, HEAD_DIM:] = jnp.ones((v_ref.shape[0], HEAD_DIM), BF16)

    def scores(head):
        kv = head // GQA_GROUP
        return _qk(q_ref[:, head * HEAD_DIM:(head + 1) * HEAD_DIM], k_ref[:, kv * HEAD_DIM:(kv + 1) * HEAD_DIM])

    s_next = scores(0)
    for head in range(N_HEADS_A):
        s = s_next
        if head + 1 < N_HEADS_A:
            s_next = scores(head + 1)
        cs = slice(head * HEAD_DIM, (head + 1) * HEAD_DIM)
        m = jnp.max(s, axis=-1, keepdims=True)
        p = jnp.exp2((s - m) * (ATTN_SCALE * LOG2E)).astype(BF16)
        ol = jnp.dot(p, v1_scr[head // GQA_GROUP], preferred_element_type=F32)
        o_ref[:, cs] = (ol[:, :HEAD_DIM] / ol[:, HEAD_DIM:]).astype(o_ref.dtype)


def _attn_a(pa, *, tq=1024):
    b, s, _ = pa.shape
    assert s % tq == 0 and PA_KA % WIDTH_KV_A == 0 and PA_VA % WIDTH_KV_A == 0
    return pl.pallas_call(
        _attn_a_kernel,
        grid=(b, s // tq),
        in_specs=[
            pl.BlockSpec((None, tq, WIDTH_A), lambda bi, i: (bi, i, PA_QA // WIDTH_A)),
            pl.BlockSpec((None, s, WIDTH_KV_A), lambda bi, i: (bi, 0, PA_KA // WIDTH_KV_A)),
            pl.BlockSpec((None, s, WIDTH_KV_A), lambda bi, i: (bi, 0, PA_VA // WIDTH_KV_A)),
        ],
        out_specs=pl.BlockSpec((None, tq, WIDTH_A), lambda bi, i: (bi, i, 0)),
        out_shape=jax.ShapeDtypeStruct((b, s, WIDTH_A), BF16),
        scratch_shapes=[pltpu.VMEM((N_KV_A, s, 2 * HEAD_DIM), BF16)],
        compiler_params=_params("parallel", "arbitrary"),
        name="attn_a",
    )(pa, pa, pa)


B_TQ = 128
B_HALF = 64
B_SKEW = 2
assert all(w // (2 * d) == B_HALF for w, d in DIL_PAIRS)


def _band_window(sub_len):
    return min(sub_len, B_TQ + 2 * B_HALF)


def _band_blocks(sub_len):
    w = _band_window(sub_len)
    nblk = sub_len // B_TQ
    out = []
    for qb in range(nblk):
        a = qb * B_TQ
        ks = min(max(a - B_HALF, 0), sub_len - w)
        case = 0 if qb == 0 else (2 if qb == nblk - 1 else 1)
        out.append((a, ks, case))
    return out


def _t5_bucket(rel):
    nb = NUM_BUCKETS // 2
    max_exact = nb // 2
    base = jnp.where(rel > 0, nb, 0)
    n = jnp.abs(rel)
    nf = jnp.maximum(n, 1).astype(F32)
    large = max_exact + (jnp.log(nf / max_exact) / math.log(MAX_DISTANCE / max_exact)
                         * (nb - max_exact)).astype(jnp.int32)
    large = jnp.minimum(large, nb - 1)
    return base + jnp.where(n < max_exact, n, large)


def _band_buckets(sub_len, dil):
    w = _band_window(sub_len)
    offsets = {}
    for a, ks, case in _band_blocks(sub_len):
        assert offsets.setdefault(case, ks - a) == ks - a
    i = np.arange(B_TQ)[:, None]
    j = np.arange(w)[None, :]
    tiles = []
    for case in sorted(offsets):
        rel = jnp.asarray(offsets[case] + j - i, jnp.int32)
        tiles.append(jnp.where(jnp.abs(rel) <= B_HALF, _t5_bucket(rel * dil), -1))
    return jnp.stack(tiles)


def _band_bias_kernel(tab_ref, bucket_ref, o_ref, *, group):
    bucket = bucket_ref[...]
    for h in range(HEADS_PER_DIL):
        acc = jnp.full(bucket.shape, NEG_INF, F32)
        for kb in range(NUM_BUCKETS):
            acc = jnp.where(bucket == kb, tab_ref[kb, group * HEADS_PER_DIL + h], acc)
        o_ref[:, h] = acc


def _band_bias(rel_bias, group, sub_len, dil):
    bucket = _band_buckets(sub_len, dil)
    ncase, tq, w = bucket.shape
    return pl.pallas_call(
        functools.partial(_band_bias_kernel, group=group),
        in_specs=[pl.BlockSpec(memory_space=pltpu.SMEM),
                  pl.BlockSpec(memory_space=pltpu.VMEM)],
        out_specs=pl.BlockSpec(memory_space=pltpu.VMEM),
        out_shape=jax.ShapeDtypeStruct((ncase, HEADS_PER_DIL, tq, w), F32),
        name=f"band_bias_dil{dil}",
    )(rel_bias.astype(F32), bucket)


def _attn_b_kernel(*refs, seq):
    qkv_refs, bias_refs = refs[:3 * N_DIL], refs[3 * N_DIL:4 * N_DIL]
    o_ref, o_scr, lse_scr = refs[4 * N_DIL:]
    tiles = [(g, dil, r, a, ks, case)
             for g, (_, dil) in enumerate(DIL_PAIRS)
             for r in range(dil)
             for a, ks, case in _band_blocks(seq // dil)]

    def rows(dil, r, start, n):
        return pl.ds(start, n) if dil == 1 else pl.ds(r + dil * start, n, stride=dil)

    def load_scores(tile):
        g, dil, r, a, ks, case = tile
        q_ref, k_ref, v_ref = qkv_refs[3 * g:3 * g + 3]
        w = _band_window(seq // dil)
        q = q_ref[rows(dil, r, a, B_TQ), :].astype(BF16)
        k = k_ref[rows(dil, r, ks, w), :].astype(BF16)
        v = v_ref[rows(dil, r, ks, w), :].astype(BF16)
        v1 = jnp.concatenate([v, jnp.ones_like(v)], axis=-1)
        return _qk(q, k) * ATTN_SCALE + bias_refs[g][case], v1

    def softmax_pv(s, v1):
        m = jnp.max(s, axis=-1, keepdims=True)
        p = jnp.exp(s - m).astype(BF16)
        return jnp.dot(p, v1, preferred_element_type=F32), m

    def store(tile, ol, m):
        g, dil, r, a, _, _ = tile
        l = ol[:, HEAD_DIM:]
        o_scr[g, rows(dil, r, a, B_TQ), :] = ol[:, :HEAD_DIM] / l
        lse_scr[g, rows(dil, r, a, B_TQ), :] = m + jnp.log(l)

    n = len(tiles)
    scored, popped = {}, {}
    for step in range(n + 2 * B_SKEW):
        if step < n:
            scored[step] = load_scores(tiles[step])
        t = step - B_SKEW
        if 0 <= t < n:
            popped[t] = softmax_pv(*scored.pop(t))
        t = step - 2 * B_SKEW
        if 0 <= t < n:
            store(tiles[t], *popped.pop(t))
    lse = [lse_scr[g] for g in range(N_DIL)]
    mx = functools.reduce(jnp.maximum, lse)
    e = [jnp.exp(x - mx) for x in lse]
    den = functools.reduce(lambda a, b: a + b, e)
    for g in range(N_DIL):
        o_ref[:, g * HEAD_DIM:(g + 1) * HEAD_DIM] = ((e[g] / den) * o_scr[g]).astype(o_ref.dtype)


def _attn_b(pa, pb, biases):
    b, s, _ = pa.shape

    def col(cols0, part, g):
        blk = (cols0 + (3 * g + part) * GROUP_W) // HEAD_DIM
        return pl.BlockSpec((None, s, HEAD_DIM), lambda bi, h: (bi, 0, blk + h))

    in_specs = [col(PA_B0, part, 0) for part in range(3)]
    operands = [pa] * 3
    for g in range(1, N_DIL):
        in_specs += [col(0, part, g - 1) for part in range(3)]
        operands += [pb] * 3
    for bias in biases:
        ncase, _, tq, w = bias.shape
        in_specs.append(pl.BlockSpec((ncase, None, tq, w), lambda bi, h: (0, h, 0, 0)))
    return pl.pallas_call(
        functools.partial(_attn_b_kernel, seq=s),
        grid=(b, HEADS_PER_DIL),
        in_specs=in_specs,
        out_specs=pl.BlockSpec((None, s, N_DIL * HEAD_DIM), lambda bi, h: (bi, 0, h)),
        out_shape=jax.ShapeDtypeStruct((b, s, WIDTH_B), BF16),
        scratch_shapes=[pltpu.VMEM((N_DIL, s, HEAD_DIM), F32), pltpu.VMEM((N_DIL, s, HEAD_DIM), F32)],
        compiler_params=_params("parallel", "parallel"),
        name="attn_b",
    )(*operands, *biases)


def _heads_major(v):
    rest = v.shape[1:]
    v = v.reshape(N_DIL, HEADS_PER_DIL, HEAD_DIM, *rest)
    return jnp.swapaxes(v, 0, 1).reshape(WIDTH_B, *rest)


def _attn_m_kernel(q_ref, kv_ref, o_ref):
    for h in range(N_HEADS_M):
        cs = slice(h * HEAD_DIM, (h + 1) * HEAD_DIM)
        vs = slice(WIDTH_M + h * HEAD_DIM, WIDTH_M + (h + 1) * HEAD_DIM)
        s = _qk(q_ref[:, cs], kv_ref[:, cs]) * ATTN_SCALE
        o, _, l = _softmax_pv(s, kv_ref[:, vs])
        o_ref[:, cs] = (o / l).astype(o_ref.dtype)


def _attn_m(pa, kv, *, tq=1024):
    b, s, _ = pa.shape
    n_mem = kv.shape[1]
    return pl.pallas_call(
        _attn_m_kernel,
        grid=(b, s // tq),
        in_specs=[
            pl.BlockSpec((None, tq, WIDTH_M), lambda bi, i: (bi, i, PA_QM // WIDTH_M)),
            pl.BlockSpec((None, n_mem, 2 * WIDTH_M), lambda bi, i: (bi, 0, 0)),
        ],
        out_specs=pl.BlockSpec((None, tq, WIDTH_M), lambda bi, i: (bi, i, 0)),
        out_shape=jax.ShapeDtypeStruct((b, s, WIDTH_M), BF16),
        compiler_params=_params("parallel", "arbitrary"),
        name="attn_m",
    )(pa, kv)


def _out_proj_kernel(oa_ref, ob_ref, om_ref, x_ref, ga_ref, gb_ref, gm_ref, w_ref, gpost_ref, o_ref):
    mix = jnp.concatenate([_rms(oa_ref[...].astype(F32), ga_ref[...]).astype(BF16),
                           _rms(ob_ref[...].astype(F32), gb_ref[...]).astype(BF16),
                           _rms(om_ref[...].astype(F32), gm_ref[...]).astype(BF16)], axis=-1)
    y = jnp.dot(mix, w_ref[...], preferred_element_type=F32)
    o_ref[...] = x_ref[...] + _rms(y, gpost_ref[...])


def _out_proj(oa, ob, om, x2d, ga, gb, gm, w, gpost, *, tm=512):
    t, d = x2d.shape
    assert t % tm == 0

    def rows(width):
        return pl.BlockSpec((tm, width), lambda i: (i, 0))

    return pl.pallas_call(
        _out_proj_kernel,
        grid=(t // tm,),
        in_specs=[rows(WIDTH_A), rows(WIDTH_B), rows(WIDTH_M), rows(d),
                  _resident(ga.shape), _resident(gb.shape), _resident(gm.shape),
                  _resident(w.shape), _resident(gpost.shape)],
        out_specs=rows(d),
        out_shape=jax.ShapeDtypeStruct((t, d), F32),
        compiler_params=_params("parallel"),
        name="out_proj",
    )(oa, ob, om, x2d, ga, gb, gm, w, gpost)


def _ffn_kernel(x_ref, gpre_ref, wup_ref, wdn_ref, gpost_ref, o_ref, h_scr):
    j = pl.program_id(1)
    last = pl.num_programs(1) - 1
    tm = x_ref.shape[0]

    def step(first, final):
        for rs in (_row_halves(tm) if first or final else (slice(0, tm),)):
            if first:
                h = _rms(x_ref[rs, :], gpre_ref[...]).astype(BF16)
                h_scr[rs, :] = h
            else:
                h = h_scr[rs, :]
            u = jnp.dot(h, wup_ref[...], preferred_element_type=F32)
            u = jnp.square(jnp.maximum(u, 0.0)).astype(BF16)
            acc = jnp.dot(u, wdn_ref[...], preferred_element_type=F32)
            if not first:
                acc = o_ref[rs, :] + acc
            if final:
                acc = x_ref[rs, :] + _rms(acc, gpost_ref[...])
            o_ref[rs, :] = acc

    pl.when(j == 0)(functools.partial(step, True, False))
    pl.when((j > 0) & (j < last))(functools.partial(step, False, False))
    pl.when(j == last)(functools.partial(step, False, True))


def _chunk_columns(w, tf):
    d, n = w.shape
    return w.reshape(d, n // tf, tf).swapaxes(0, 1)


def _ffn(x2d, gpre, w_up, w_down, gpost, *, tm=1024):
    t, d = x2d.shape
    n_steps, _, tf = w_up.shape
    assert t % tm == 0 and n_steps >= 2 and w_down.shape == (n_steps * tf, d)
    return pl.pallas_call(
        _ffn_kernel,
        grid=(t // tm, n_steps),
        in_specs=[
            pl.BlockSpec((tm, d), lambda i, j: (i, 0)),
            _resident(gpre.shape),
            pl.BlockSpec((None, d, tf), lambda i, j: (j, 0, 0)),
            pl.BlockSpec((tf, d), lambda i, j: (j, 0)),
            _resident(gpost.shape),
        ],
        out_specs=pl.BlockSpec((tm, d), lambda i, j: (i, 0)),
        out_shape=jax.ShapeDtypeStruct((t, d), F32),
        scratch_shapes=[pltpu.VMEM((tm, d), BF16)],
        compiler_params=_params("parallel", "arbitrary"),
        name="ffn",
    )(x2d, gpre, w_up, w_down, gpost)


def _rope_tables(seq_len):
    rows = seq_len // GRID_W
    row = jnp.repeat(jnp.arange(rows), GRID_W).astype(F32)
    col = jnp.tile(jnp.arange(GRID_W), rows).astype(F32)
    inv = ROPE_THETA ** (-(2.0 * jnp.arange(ROPE_PAIRS, dtype=F32)) / ROPE_AXIS_DIM)
    ang_r = row[:, None] * inv
    ang_c = col[:, None] * inv
    cos = jnp.concatenate([jnp.cos(ang_r)] * 2 + [jnp.cos(ang_c)] * 2, axis=-1)
    sin = jnp.concatenate([-jnp.sin(ang_r), jnp.sin(ang_r), -jnp.sin(ang_c), jnp.sin(ang_c)], axis=-1)
    return cos, sin


def _row(v):
    return v.reshape(1, -1).astype(F32)


def _permute_w_in(w):
    splits = [int(c) for c in np.cumsum([WIDTH_A, WIDTH_KV_A, WIDTH_KV_A, WIDTH_B, WIDTH_B, WIDTH_B])]
    qa, ka, va, qb, kb, vb, qm = jnp.split(w, splits, axis=1)

    def grp(m, g):
        return m[:, g * GROUP_W:(g + 1) * GROUP_W]

    parts = [qa, ka, qm, va] + [grp(m, 0) for m in (qb, kb, vb)]
    for g in range(1, N_DIL):
        parts += [grp(m, g) for m in (qb, kb, vb)]
    return jnp.concatenate(parts, axis=1)


def _encoder_layer(x, mem, p, biases):
    b, s, d = x.shape
    x2d = x.reshape(b * s, d)
    cos_tab, sin_tab = _rope_tables(s)
    pa, pb = _in_proj(x2d, p["pre_mix_norm"], p["w_in"], p["head_gain"], cos_tab, sin_tab, seq=s)
    pa = pa.reshape(b, s, PA_COLS)
    pb = pb.reshape(b, s, PB_COLS)
    kv = _mem_kv(mem.reshape(-1, d), p["mem_norm"], p["w_mem_kv"]).reshape(b, mem.shape[1], -1)

    oa = _attn_a(pa).reshape(b * s, WIDTH_A)
    ob = _attn_b(pa, pb, biases).reshape(b * s, WIDTH_B)
    om = _attn_m(pa, kv).reshape(b * s, WIDTH_M)

    x1 = _out_proj(oa, ob, om, x2d, p["out_norm_a"], p["out_norm_b"], p["out_norm_m"],
                   p["w_out"], p["post_mix_norm"])
    y = _ffn(x1, p["pre_ffn_norm"], p["w_up"], p["w_down"], p["post_ffn_norm"])
    return y.reshape(b, s, d)


def kernel(x_prompt, x_sample, mem_prompt, mem_sample, rel_bias, pre_mix_norm, w_in, q_norm_a, k_norm_a, mem_norm, w_mem_kv, out_norm_a, out_norm_b, out_norm_m, w_out, post_mix_norm, pre_ffn_norm, w_up, w_down, post_ffn_norm):
    depth = w_in.shape[0]
    assert x_prompt.shape[1] == x_sample.shape[1]
    seq = x_prompt.shape[1]
    biases = [_band_bias(rel_bias, g, seq // dil, dil) for g, (_, dil) in enumerate(DIL_PAIRS)]
    layers = []
    for l in range(depth):
        w_o = w_out[l]
        w_o = jnp.concatenate([w_o[:WIDTH_A], _heads_major(w_o[WIDTH_A:WIDTH_A + WIDTH_B]),
                               w_o[WIDTH_A + WIDTH_B:]], axis=0)
        layers.append({
            "pre_mix_norm": _row(pre_mix_norm[l]),
            "w_in": _permute_w_in(w_in[l]).astype(BF16),
            "head_gain": _row(jnp.concatenate([jnp.tile(q_norm_a[l], N_HEADS_A),
                                               jnp.tile(k_norm_a[l], N_KV_A)])),
            "mem_norm": _row(mem_norm[l]),
            "w_mem_kv": w_mem_kv[l].astype(BF16),
            "out_norm_a": _row(out_norm_a[l]),
            "out_norm_b": _row(_heads_major(out_norm_b[l])),
            "out_norm_m": _row(out_norm_m[l]),
            "w_out": w_o.astype(BF16),
            "post_mix_norm": _row(post_mix_norm[l]),
            "pre_ffn_norm": _row(pre_ffn_norm[l]),
            "w_up": _chunk_columns(w_up[l].astype(BF16), FFN_CHUNK),
            "w_down": w_down[l].astype(BF16),
            "post_ffn_norm": _row(post_ffn_norm[l]),
        })

    def trunk(x, mem):
        for p in layers:
            x = _encoder_layer(x, mem, p, biases)
        return x

    return (trunk(x_prompt, mem_prompt), trunk(x_sample, mem_sample))
```

```python
import functools
import math

import jax
import jax.numpy as jnp
import numpy as np
from jax import lax
from jax.experimental import pallas as pl
from jax.experimental.pallas import tpu as pltpu

F32 = jnp.float32
BF16 = jnp.bfloat16

HEAD_DIM = 128
N_HEADS_A = 6
N_KV_A = 2
GQA_GROUP = N_HEADS_A // N_KV_A
DIL_PAIRS = ((128, 1), (512, 4), (2048, 16))
N_DIL = len(DIL_PAIRS)
HEADS_PER_DIL = 2
N_HEADS_B = N_DIL * HEADS_PER_DIL
N_HEADS_M = 4
WIDTH_A = N_HEADS_A * HEAD_DIM
WIDTH_KV_A = N_KV_A * HEAD_DIM
WIDTH_B = N_HEADS_B * HEAD_DIM
WIDTH_M = N_HEADS_M * HEAD_DIM
GROUP_W = HEADS_PER_DIL * HEAD_DIM
GRID_W = 64
ROPE_THETA = 10000.0
ROPE_AXIS_DIM = HEAD_DIM // 2
ROPE_PAIRS = ROPE_AXIS_DIM // 2
NUM_BUCKETS = 32
MAX_DISTANCE = 1024
EPS = 1e-6
NEG_INF = -1e30
ATTN_SCALE = HEAD_DIM ** -0.5
LOG2E = math.log2(math.e)

PA_QA = 0
PA_KA = PA_QA + WIDTH_A
ROPE_COLS = PA_KA + WIDTH_KV_A
PA_QM = ROPE_COLS
PA_VA = PA_QM + WIDTH_M
PA_B0 = PA_VA + WIDTH_KV_A
PA_COLS = PA_B0 + 3 * GROUP_W
PB_COLS = (N_DIL - 1) * 3 * GROUP_W
assert PA_QA % (GQA_GROUP * HEAD_DIM) == 0 and PA_QM % WIDTH_M == 0
IN_COLS = PA_COLS + PB_COLS
IN_CHUNK = 512
FFN_CHUNK = 1024

V7X_VMEM_BYTES = 64 * 1024 * 1024
VMEM_LIMIT_BYTES = V7X_VMEM_BYTES - 4 * 1024 * 1024


def _params(*semantics):
    return pltpu.CompilerParams(dimension_semantics=semantics,
                                vmem_limit_bytes=VMEM_LIMIT_BYTES)


def _resident(shape):
    zeros = (0,) * len(shape)
    return pl.BlockSpec(shape, lambda *_: zeros, pipeline_mode=pl.Buffered(1))


def _rms(v, g):
    ms = jnp.mean(v * v, axis=-1, keepdims=True)
    return v * lax.rsqrt(ms + EPS) * g


def _row_halves(rows):
    return (slice(0, rows // 2), slice(rows // 2, rows))


def _qk(q, k):
    return lax.dot_general(q, k, (((1,), (1,)), ((), ())), preferred_element_type=F32)


def _in_proj_kernel(x_ref, g_ref, w_ref, hg_ref, cos_ref, sin_ref, pa_ref, pb_ref):
    h = _rms(x_ref[...], g_ref[...]).astype(BF16)
    tm = h.shape[0]
    cos = cos_ref[...]
    sin = sin_ref[...]
    lane = lax.broadcasted_iota(jnp.int32, (tm, HEAD_DIM), 1)
    low_half = (lane % ROPE_AXIS_DIM) < ROPE_PAIRS
    for c0 in range(0, IN_COLS, IN_CHUNK):
        acc = jnp.dot(h, w_ref[:, c0:c0 + IN_CHUNK], preferred_element_type=F32)
        if c0 < ROPE_COLS:
            for hh in range(IN_CHUNK // HEAD_DIM):
                cs = slice(c0 + hh * HEAD_DIM, c0 + (hh + 1) * HEAD_DIM)
                y = _rms(acc[:, hh * HEAD_DIM:(hh + 1) * HEAD_DIM], hg_ref[:, cs])
                partner = jnp.where(low_half, pltpu.roll(y, HEAD_DIM - ROPE_PAIRS, 1),
                                    pltpu.roll(y, ROPE_PAIRS, 1))
                pa_ref[:, cs] = (y * cos + partner * sin).astype(BF16)
        elif c0 < PA_COLS:
            pa_ref[:, c0:c0 + IN_CHUNK] = acc.astype(BF16)
        else:
            pb_ref[:, c0 - PA_COLS:c0 - PA_COLS + IN_CHUNK] = acc


def _in_proj(x2d, gain, w, head_gain, cos_tab, sin_tab, *, seq, tm=512):
    t, d = x2d.shape
    assert t % tm == 0 and seq % tm == 0
    assert ROPE_COLS % IN_CHUNK == 0 and PA_COLS % IN_CHUNK == 0 and IN_COLS % IN_CHUNK == 0
    pos_blocks = seq // tm
    return pl.pallas_call(
        _in_proj_kernel,
        grid=(t // tm,),
        in_specs=[
            pl.BlockSpec((tm, d), lambda i: (i, 0)),
            _resident(gain.shape),
            _resident(w.shape),
            _resident(head_gain.shape),
            pl.BlockSpec((tm, HEAD_DIM), lambda i: (i % pos_blocks, 0)),
            pl.BlockSpec((tm, HEAD_DIM), lambda i: (i % pos_blocks, 0)),
        ],
        out_specs=[pl.BlockSpec((tm, PA_COLS), lambda i: (i, 0)),
                   pl.BlockSpec((tm, PB_COLS), lambda i: (i, 0))],
        out_shape=[jax.ShapeDtypeStruct((t, PA_COLS), BF16),
                   jax.ShapeDtypeStruct((t, PB_COLS), F32)],
        compiler_params=_params("parallel"),
        name="in_proj",
    )(x2d, gain, w, head_gain, cos_tab, sin_tab)


def _norm_matmul_kernel(x_ref, g_ref, w_ref, o_ref, h_scr):
    @pl.when(pl.program_id(1) == 0)
    def _():
        h_scr[...] = _rms(x_ref[...], g_ref[...]).astype(BF16)

    o_ref[...] = jnp.dot(h_scr[...], w_ref[...], preferred_element_type=F32).astype(o_ref.dtype)


def _mem_kv(mem2d, gain, w, *, tm=1024, tn=1024):
    t, d = mem2d.shape
    n = w.shape[1]
    assert t % tm == 0 and n % tn == 0
    return pl.pallas_call(
        _norm_matmul_kernel,
        grid=(t // tm, n // tn),
        in_specs=[
            pl.BlockSpec((tm, d), lambda i, j: (i, 0)),
            pl.BlockSpec((1, d), lambda i, j: (0, 0)),
            pl.BlockSpec((d, tn), lambda i, j: (0, j)),
        ],
        out_specs=pl.BlockSpec((tm, tn), lambda i, j: (i, j)),
        out_shape=jax.ShapeDtypeStruct((t, n), BF16),
        scratch_shapes=[pltpu.VMEM((tm, d), BF16)],
        compiler_params=_params("parallel", "arbitrary"),
        name="mem_kv",
    )(mem2d, gain, w)


def _attn_am_kernel(qa_ref, ka_ref, va_ref, qm_ref, kvm_ref, oa_ref, om_ref, va1_scr, vm1_scr):
    @pl.when(pl.program_id(1) == 0)
    def _():
        for kv in range(N_KV_A):
            va1_scr[kv, :, :HEAD_DIM] = va_ref[:, kv * HEAD_DIM:(kv + 1) * HEAD_DIM]
            va1_scr[kv, :, HEAD_DIM:] = jnp.ones((va_ref.shape[0], HEAD_DIM), BF16)
        for h in range(N_HEADS_M):
            vm1_scr[h, :, :HEAD_DIM] = kvm_ref[:, WIDTH_M + h * HEAD_DIM:WIDTH_M + (h + 1) * HEAD_DIM]
            vm1_scr[h, :, HEAD_DIM:] = jnp.ones((kvm_ref.shape[0], HEAD_DIM), BF16)

    def head_cols(h):
        return slice(h * HEAD_DIM, (h + 1) * HEAD_DIM)

    units = [(qa_ref, ka_ref, head_cols(h // GQA_GROUP), (va1_scr, h // GQA_GROUP), oa_ref, head_cols(h))
             for h in range(N_HEADS_A)]
    units += [(qm_ref, kvm_ref, head_cols(h), (vm1_scr, h), om_ref, head_cols(h)) for h in range(N_HEADS_M)]

    def scores(unit):
        q_ref, k_ref, kcols, _, _, cs = unit
        return _qk(q_ref[:, cs], k_ref[:, kcols])

    s_next = scores(units[0])
    for idx, unit in enumerate(units):
        s = s_next
        if idx + 1 < len(units):
            s_next = scores(units[idx + 1])
        _, _, _, (v1_scr, vi), o_ref, cs = unit
        m = jnp.max(s, axis=-1, keepdims=True)
        p = jnp.exp2((s - m) * (ATTN_SCALE * LOG2E)).astype(BF16)
        ol = jnp.dot(p, v1_scr[vi], preferred_element_type=F32)
        o_ref[:, cs] = (ol[:, :HEAD_DIM] / ol[:, HEAD_DIM:]).astype(o_ref.dtype)


def _attn_am(pa, kvm, *, tq=1024):
    b, s, _ = pa.shape
    n_mem = kvm.shape[1]
    assert s % tq == 0 and PA_KA % WIDTH_KV_A == 0 and PA_VA % WIDTH_KV_A == 0
    return pl.pallas_call(
        _attn_am_kernel,
        grid=(b, s // tq),
        in_specs=[
            pl.BlockSpec((None, tq, WIDTH_A), lambda bi, i: (bi, i, PA_QA // WIDTH_A)),
            pl.BlockSpec((None, s, WIDTH_KV_A), lambda bi, i: (bi, 0, PA_KA // WIDTH_KV_A)),
            pl.BlockSpec((None, s, WIDTH_KV_A), lambda bi, i: (bi, 0, PA_VA // WIDTH_KV_A)),
            pl.BlockSpec((None, tq, WIDTH_M), lambda bi, i: (bi, i, PA_QM // WIDTH_M)),
            pl.BlockSpec((None, n_mem, 2 * WIDTH_M), lambda bi, i: (bi, 0, 0)),
        ],
        out_specs=[pl.BlockSpec((None, tq, WIDTH_A), lambda bi, i: (bi, i, 0)),
                   pl.BlockSpec((None, tq, WIDTH_M), lambda bi, i: (bi, i, 0))],
        out_shape=[jax.ShapeDtypeStruct((b, s, WIDTH_A), BF16),
                   jax.ShapeDtypeStruct((b, s, WIDTH_M), BF16)],
        scratch_shapes=[pltpu.VMEM((N_KV_A, s, 2 * HEAD_DIM), BF16),
                        pltpu.VMEM((N_HEADS_M, n_mem, 2 * HEAD_DIM), BF16)],
        compiler_params=_params("parallel", "arbitrary"),
        name="attn_am",
    )(pa, pa, pa, pa, kvm)


B_TQ = 128
B_HALF = 64
B_SKEW = 2
assert all(w // (2 * d) == B_HALF for w, d in DIL_PAIRS)


def _band_window(sub_len):
    return min(sub_len, B_TQ + 2 * B_HALF)


def _band_blocks(sub_len):
    w = _band_window(sub_len)
    nblk = sub_len // B_TQ
    out = []
    for qb in range(nblk):
        a = qb * B_TQ
        ks = min(max(a - B_HALF, 0), sub_len - w)
        case = 0 if qb == 0 else (2 if qb == nblk - 1 else 1)
        out.append((a, ks, case))
    return out


def _t5_bucket(rel):
    nb = NUM_BUCKETS // 2
    max_exact = nb // 2
    base = jnp.where(rel > 0, nb, 0)
    n = jnp.abs(rel)
    nf = jnp.maximum(n, 1).astype(F32)
    large = max_exact + (jnp.log(nf / max_exact) / math.log(MAX_DISTANCE / max_exact)
                         * (nb - max_exact)).astype(jnp.int32)
    large = jnp.minimum(large, nb - 1)
    return base + jnp.where(n < max_exact, n, large)


def _band_buckets(sub_len, dil):
    w = _band_window(sub_len)
    offsets = {}
    for a, ks, case in _band_blocks(sub_len):
        assert offsets.setdefault(case, ks - a) == ks - a
    i = np.arange(B_TQ)[:, None]
    j = np.arange(w)[None, :]
    tiles = []
    for case in sorted(offsets):
        rel = jnp.asarray(offsets[case] + j - i, jnp.int32)
        tiles.append(jnp.where(jnp.abs(rel) <= B_HALF, _t5_bucket(rel * dil), -1))
    return jnp.stack(tiles)


def _band_bias_kernel(tab_ref, bucket_ref, o_ref, *, group):
    bucket = bucket_ref[...]
    for h in range(HEADS_PER_DIL):
        acc = jnp.full(bucket.shape, NEG_INF, F32)
        for kb in range(NUM_BUCKETS):
            acc = jnp.where(bucket == kb, tab_ref[kb, group * HEADS_PER_DIL + h], acc)
        o_ref[:, h] = acc


def _band_bias(rel_bias, group, sub_len, dil):
    bucket = _band_buckets(sub_len, dil)
    ncase, tq, w = bucket.shape
    return pl.pallas_call(
        functools.partial(_band_bias_kernel, group=group),
        in_specs=[pl.BlockSpec(memory_space=pltpu.SMEM),
                  pl.BlockSpec(memory_space=pltpu.VMEM)],
        out_specs=pl.BlockSpec(memory_space=pltpu.VMEM),
        out_shape=jax.ShapeDtypeStruct((ncase, HEADS_PER_DIL, tq, w), F32),
        name=f"band_bias_dil{dil}",
    )(rel_bias.astype(F32), bucket)


def _attn_b_kernel(*refs, seq):
    qkv_refs, bias_refs = refs[:3 * N_DIL], refs[3 * N_DIL:4 * N_DIL]
    o_ref, o_scr, lse_scr = refs[4 * N_DIL:]
    tiles = [(g, dil, r, a, ks, case)
             for g, (_, dil) in enumerate(DIL_PAIRS)
             for r in range(dil)
             for a, ks, case in _band_blocks(seq // dil)]

    def rows(dil, r, start, n):
        return pl.ds(start, n) if dil == 1 else pl.ds(r + dil * start, n, stride=dil)

    def load_scores(tile):
        g, dil, r, a, ks, case = tile
        q_ref, k_ref, v_ref = qkv_refs[3 * g:3 * g + 3]
        w = _band_window(seq // dil)
        q = q_ref[rows(dil, r, a, B_TQ), :].astype(BF16)
        k = k_ref[rows(dil, r, ks, w), :].astype(BF16)
        v = v_ref[rows(dil, r, ks, w), :].astype(BF16)
        v1 = jnp.concatenate([v, jnp.ones_like(v)], axis=-1)
        return _qk(q, k) * ATTN_SCALE + bias_refs[g][case], v1

    def softmax_pv(s, v1):
        m = jnp.max(s, axis=-1, keepdims=True)
        p = jnp.exp(s - m).astype(BF16)
        return jnp.dot(p, v1, preferred_element_type=F32), m

    def store(tile, ol, m):
        g, dil, r, a, _, _ = tile
        l = ol[:, HEAD_DIM:]
        o_scr[g, rows(dil, r, a, B_TQ), :] = ol[:, :HEAD_DIM] / l
        lse_scr[g, rows(dil, r, a, B_TQ), :] = m + jnp.log(l)

    n = len(tiles)
    scored, popped = {}, {}
    for step in range(n + 2 * B_SKEW):
        if step < n:
            scored[step] = load_scores(tiles[step])
        t = step - B_SKEW
        if 0 <= t < n:
            popped[t] = softmax_pv(*scored.pop(t))
        t = step - 2 * B_SKEW
        if 0 <= t < n:
            store(tiles[t], *popped.pop(t))
    lse = [lse_scr[g] for g in range(N_DIL)]
    mx = functools.reduce(jnp.maximum, lse)
    e = [jnp.exp(x - mx) for x in lse]
    den = functools.reduce(lambda a, b: a + b, e)
    for g in range(N_DIL):
        o_ref[:, g * HEAD_DIM:(g + 1) * HEAD_DIM] = ((e[g] / den) * o_scr[g]).astype(o_ref.dtype)


def _attn_b(pa, pb, biases):
    b, s, _ = pa.shape

    def col(cols0, part, g):
        blk = (cols0 + (3 * g + part) * GROUP_W) // HEAD_DIM
        return pl.BlockSpec((None, s, HEAD_DIM), lambda bi, h: (bi, 0, blk + h))

    in_specs = [col(PA_B0, part, 0) for part in range(3)]
    operands = [pa] * 3
    for g in range(1, N_DIL):
        in_specs += [col(0, part, g - 1) for part in range(3)]
        operands += [pb] * 3
    for bias in biases:
        ncase, _, tq, w = bias.shape
        in_specs.append(pl.BlockSpec((ncase, None, tq, w), lambda bi, h: (0, h, 0, 0)))
    return pl.pallas_call(
        functools.partial(_attn_b_kernel, seq=s),
        grid=(b, HEADS_PER_DIL),
        in_specs=in_specs,
        out_specs=pl.BlockSpec((None, s, N_DIL * HEAD_DIM), lambda bi, h: (bi, 0, h)),
        out_shape=jax.ShapeDtypeStruct((b, s, WIDTH_B), BF16),
        scratch_shapes=[pltpu.VMEM((N_DIL, s, HEAD_DIM), F32), pltpu.VMEM((N_DIL, s, HEAD_DIM), F32)],
        compiler_params=_params("parallel", "parallel"),
        name="attn_b",
    )(*operands, *biases)


def _heads_major(v):
    rest = v.shape[1:]
    v = v.reshape(N_DIL, HEADS_PER_DIL, HEAD_DIM, *rest)
    return jnp.swapaxes(v, 0, 1).reshape(WIDTH_B, *rest)


def _out_proj_kernel(oa_ref, ob_ref, om_ref, x_ref, ga_ref, gb_ref, gm_ref, w_ref, gpost_ref, o_ref):
    mix = jnp.concatenate([_rms(oa_ref[...].astype(F32), ga_ref[...]).astype(BF16),
                           _rms(ob_ref[...].astype(F32), gb_ref[...]).astype(BF16),
                           _rms(om_ref[...].astype(F32), gm_ref[...]).astype(BF16)], axis=-1)
    y = jnp.dot(mix, w_ref[...], preferred_element_type=F32)
    o_ref[...] = x_ref[...] + _rms(y, gpost_ref[...])


def _out_proj(oa, ob, om, x2d, ga, gb, gm, w, gpost, *, tm=512):
    t, d = x2d.shape
    assert t % tm == 0

    def rows(width):
        return pl.BlockSpec((tm, width), lambda i: (i, 0))

    return pl.pallas_call(
        _out_proj_kernel,
        grid=(t // tm,),
        in_specs=[rows(WIDTH_A), rows(WIDTH_B), rows(WIDTH_M), rows(d),
                  _resident(ga.shape), _resident(gb.shape), _resident(gm.shape),
                  _resident(w.shape), _resident(gpost.shape)],
        out_specs=rows(d),
        out_shape=jax.ShapeDtypeStruct((t, d), F32),
        compiler_params=_params("parallel"),
        name="out_proj",
    )(oa, ob, om, x2d, ga, gb, gm, w, gpost)


def _ffn_kernel(x_ref, gpre_ref, wup_ref, wdn_ref, gpost_ref, o_ref, h_scr):
    j = pl.program_id(1)
    last = pl.num_programs(1) - 1
    tm = x_ref.shape[0]

    def step(first, final):
        for rs in (_row_halves(tm) if first or final else (slice(0, tm),)):
            if first:
                h = _rms(x_ref[rs, :], gpre_ref[...]).astype(BF16)
                h_scr[rs, :] = h
            else:
                h = h_scr[rs, :]
            u = jnp.dot(h, wup_ref[...], preferred_element_type=F32)
            u = jnp.square(jnp.maximum(u, 0.0)).astype(BF16)
            acc = jnp.dot(u, wdn_ref[...], preferred_element_type=F32)
            if not first:
                acc = o_ref[rs, :] + acc
            if final:
                acc = x_ref[rs, :] + _rms(acc, gpost_ref[...])
            o_ref[rs, :] = acc

    pl.when(j == 0)(functools.partial(step, True, False))
    pl.when((j > 0) & (j < last))(functools.partial(step, False, False))
    pl.when(j == last)(functools.partial(step, False, True))


def _ffn(x2d, gpre, w_up, w_down, gpost, *, tm=1024, tf=FFN_CHUNK):
    t, d = x2d.shape
    d_ff = w_up.shape[1]
    assert t % tm == 0 and d_ff % tf == 0 and d_ff // tf >= 2
    return pl.pallas_call(
        _ffn_kernel,
        grid=(t // tm, d_ff // tf),
        in_specs=[
            pl.BlockSpec((tm, d), lambda i, j: (i, 0)),
            _resident(gpre.shape),
            pl.BlockSpec((d, tf), lambda i, j: (0, j)),
            pl.BlockSpec((tf, d), lambda i, j: (j, 0)),
            _resident(gpost.shape),
        ],
        out_specs=pl.BlockSpec((tm, d), lambda i, j: (i, 0)),
        out_shape=jax.ShapeDtypeStruct((t, d), F32),
        scratch_shapes=[pltpu.VMEM((tm, d), BF16)],
        compiler_params=_params("parallel", "arbitrary"),
        name="ffn",
    )(x2d, gpre, w_up, w_down, gpost)


def _rope_tables(seq_len):
    rows = seq_len // GRID_W
    row = jnp.repeat(jnp.arange(rows), GRID_W).astype(F32)
    col = jnp.tile(jnp.arange(GRID_W), rows).astype(F32)
    inv = ROPE_THETA ** (-(2.0 * jnp.arange(ROPE_PAIRS, dtype=F32)) / ROPE_AXIS_DIM)
    ang_r = row[:, None] * inv
    ang_c = col[:, None] * inv
    cos = jnp.concatenate([jnp.cos(ang_r)] * 2 + [jnp.cos(ang_c)] * 2, axis=-1)
    sin = jnp.concatenate([-jnp.sin(ang_r), jnp.sin(ang_r), -jnp.sin(ang_c), jnp.sin(ang_c)], axis=-1)
    return cos, sin


def _row(v):
    return v.reshape(1, -1).astype(F32)


def _permute_w_in(w):
    splits = [int(c) for c in np.cumsum([WIDTH_A, WIDTH_KV_A, WIDTH_KV_A, WIDTH_B, WIDTH_B, WIDTH_B])]
    qa, ka, va, qb, kb, vb, qm = jnp.split(w, splits, axis=1)

    def grp(m, g):
        return m[:, g * GROUP_W:(g + 1) * GROUP_W]

    parts = [qa, ka, qm, va] + [grp(m, 0) for m in (qb, kb, vb)]
    for g in range(1, N_DIL):
        parts += [grp(m, g) for m in (qb, kb, vb)]
    return jnp.concatenate(parts, axis=1)


def _encoder_layer(x, mem, p, biases):
    b, s, d = x.shape
    x2d = x.reshape(b * s, d)
    cos_tab, sin_tab = _rope_tables(s)
    pa, pb = _in_proj(x2d, p["pre_mix_norm"], p["w_in"], p["head_gain"], cos_tab, sin_tab, seq=s)
    pa = pa.reshape(b, s, PA_COLS)
    pb = pb.reshape(b, s, PB_COLS)
    kv = _mem_kv(mem.reshape(-1, d), p["mem_norm"], p["w_mem_kv"]).reshape(b, mem.shape[1], -1)

    oa, om = _attn_am(pa, kv)
    ob = _attn_b(pa, pb, biases)

    x1 = _out_proj(oa.reshape(b * s, WIDTH_A), ob.reshape(b * s, WIDTH_B), om.reshape(b * s, WIDTH_M), x2d,
                   p["out_norm_a"], p["out_norm_b"], p["out_norm_m"], p["w_out"], p["post_mix_norm"])
    y = _ffn(x1, p["pre_ffn_norm"], p["w_up"], p["w_down"], p["post_ffn_norm"])
    return y.reshape(b, s, d)


def kernel(x_prompt, x_sample, mem_prompt, mem_sample, rel_bias, pre_mix_norm, w_in, q_norm_a, k_norm_a, mem_norm, w_mem_kv, out_norm_a, out_norm_b, out_norm_m, w_out, post_mix_norm, pre_ffn_norm, w_up, w_down, post_ffn_norm):
    depth = w_in.shape[0]
    assert x_prompt.shape[1] == x_sample.shape[1]
    seq = x_prompt.shape[1]
    biases = [_band_bias(rel_bias, g, seq // dil, dil) for g, (_, dil) in enumerate(DIL_PAIRS)]
    layers = []
    for l in range(depth):
        w_o = w_out[l]
        w_o = jnp.concatenate([w_o[:WIDTH_A], _heads_major(w_o[WIDTH_A:WIDTH_A + WIDTH_B]),
                               w_o[WIDTH_A + WIDTH_B:]], axis=0)
        layers.append({
            "pre_mix_norm": _row(pre_mix_norm[l]),
            "w_in": _permute_w_in(w_in[l]).astype(BF16),
            "head_gain": _row(jnp.concatenate([jnp.tile(q_norm_a[l], N_HEADS_A),
                                               jnp.tile(k_norm_a[l], N_KV_A)])),
            "mem_norm": _row(mem_norm[l]),
            "w_mem_kv": w_mem_kv[l].astype(BF16),
            "out_norm_a": _row(out_norm_a[l]),
            "out_norm_b": _row(_heads_major(out_norm_b[l])),
            "out_norm_m": _row(out_norm_m[l]),
            "w_out": w_o.astype(BF16),
            "post_mix_norm": _row(post_mix_norm[l]),
            "pre_ffn_norm": _row(pre_ffn_norm[l]),
            "w_up": w_up[l].astype(BF16),
            "w_down": w_down[l].astype(BF16),
            "post_ffn_norm": _row(post_ffn_norm[l]),
        })

    def trunk(x, mem):
        for p in layers:
            x = _encoder_layer(x, mem, p, biases)
        return x

    return (trunk(x_prompt, mem_prompt), trunk(x_sample, mem_sample))
```

```python
import functools
import math

import jax
import jax.numpy as jnp
import numpy as np
from jax import lax
from jax.experimental import pallas as pl
from jax.experimental.pallas import tpu as pltpu

F32 = jnp.float32
BF16 = jnp.bfloat16

HEAD_DIM = 128
N_HEADS_A = 6
N_KV_A = 2
GQA_GROUP = N_HEADS_A // N_KV_A
DIL_PAIRS = ((128, 1), (512, 4), (2048, 16))
N_DIL = len(DIL_PAIRS)
HEADS_PER_DIL = 2
N_HEADS_B = N_DIL * HEADS_PER_DIL
N_HEADS_M = 4
WIDTH_A = N_HEADS_A * HEAD_DIM
WIDTH_KV_A = N_KV_A * HEAD_DIM
WIDTH_B = N_HEADS_B * HEAD_DIM
WIDTH_M = N_HEADS_M * HEAD_DIM
GROUP_W = HEADS_PER_DIL * HEAD_DIM
GRID_W = 64
ROPE_THETA = 10000.0
ROPE_AXIS_DIM = HEAD_DIM // 2
ROPE_PAIRS = ROPE_AXIS_DIM // 2
NUM_BUCKETS = 32
MAX_DISTANCE = 1024
EPS = 1e-6
NEG_INF = -1e30
ATTN_SCALE = HEAD_DIM ** -0.5
LOG2E = math.log2(math.e)

PA_QA = 0
PA_KA = PA_QA + WIDTH_A
ROPE_COLS = PA_KA + WIDTH_KV_A
PA_QM = ROPE_COLS
PA_VA = PA_QM + WIDTH_M
PA_B0 = PA_VA + WIDTH_KV_A
PA_COLS = PA_B0 + 3 * GROUP_W
PB_COLS = (N_DIL - 1) * 3 * GROUP_W
assert PA_QA % (GQA_GROUP * HEAD_DIM) == 0 and PA_QM % WIDTH_M == 0
IN_COLS = PA_COLS + PB_COLS
IN_CHUNK = 512
FFN_CHUNK = 1024

V7X_VMEM_BYTES = 64 * 1024 * 1024
VMEM_LIMIT_BYTES = V7X_VMEM_BYTES - 4 * 1024 * 1024


def _params(*semantics):
    return pltpu.CompilerParams(dimension_semantics=semantics,
                                vmem_limit_bytes=VMEM_LIMIT_BYTES)


def _resident(shape):
    zeros = (0,) * len(shape)
    return pl.BlockSpec(shape, lambda *_: zeros, pipeline_mode=pl.Buffered(1))


def _rms(v, g):
    ms = jnp.mean(v * v, axis=-1, keepdims=True)
    return v * lax.rsqrt(ms + EPS) * g


def _row_halves(rows):
    return (slice(0, rows // 2), slice(rows // 2, rows))


def _qk(q, k):
    return lax.dot_general(q, k, (((1,), (1,)), ((), ())), preferred_element_type=F32)


def _in_proj_kernel(x_ref, g_ref, w_ref, hg_ref, cos_ref, sin_ref, pa_ref, pb_ref):
    h = _rms(x_ref[...], g_ref[...]).astype(BF16)
    tm = h.shape[0]
    cos = cos_ref[...]
    sin = sin_ref[...]
    lane = lax.broadcasted_iota(jnp.int32, (tm, HEAD_DIM), 1)
    low_half = (lane % ROPE_AXIS_DIM) < ROPE_PAIRS
    for c0 in range(0, IN_COLS, IN_CHUNK):
        acc = jnp.dot(h, w_ref[:, c0:c0 + IN_CHUNK], preferred_element_type=F32)
        if c0 < ROPE_COLS:
            for hh in range(IN_CHUNK // HEAD_DIM):
                cs = slice(c0 + hh * HEAD_DIM, c0 + (hh + 1) * HEAD_DIM)
                y = _rms(acc[:, hh * HEAD_DIM:(hh + 1) * HEAD_DIM], hg_ref[:, cs])
                partner = jnp.where(low_half, pltpu.roll(y, HEAD_DIM - ROPE_PAIRS, 1),
                                    pltpu.roll(y, ROPE_PAIRS, 1))
                pa_ref[:, cs] = (y * cos + partner * sin).astype(BF16)
        elif c0 < PA_COLS:
            pa_ref[:, c0:c0 + IN_CHUNK] = acc.astype(BF16)
        else:
            pb_ref[:, c0 - PA_COLS:c0 - PA_COLS + IN_CHUNK] = acc


def _in_proj(x2d, gain, w, head_gain, cos_tab, sin_tab, *, seq, tm=512):
    t, d = x2d.shape
    assert t % tm == 0 and seq % tm == 0
    assert ROPE_COLS % IN_CHUNK == 0 and PA_COLS % IN_CHUNK == 0 and IN_COLS % IN_CHUNK == 0
    pos_blocks = seq // tm
    return pl.pallas_call(
        _in_proj_kernel,
        grid=(t // tm,),
        in_specs=[
            pl.BlockSpec((tm, d), lambda i: (i, 0)),
            _resident(gain.shape),
            _resident(w.shape),
            _resident(head_gain.shape),
            pl.BlockSpec((tm, HEAD_DIM), lambda i: (i % pos_blocks, 0)),
            pl.BlockSpec((tm, HEAD_DIM), lambda i: (i % pos_blocks, 0)),
        ],
        out_specs=[pl.BlockSpec((tm, PA_COLS), lambda i: (i, 0)),
                   pl.BlockSpec((tm, PB_COLS), lambda i: (i, 0))],
        out_shape=[jax.ShapeDtypeStruct((t, PA_COLS), BF16),
                   jax.ShapeDtypeStruct((t, PB_COLS), F32)],
        compiler_params=_params("parallel"),
        name="in_proj",
    )(x2d, gain, w, head_gain, cos_tab, sin_tab)


def _norm_matmul_kernel(x_ref, g_ref, w_ref, o_ref, h_scr):
    @pl.when(pl.program_id(1) == 0)
    def _():
        h_scr[...] = _rms(x_ref[...], g_ref[...]).astype(BF16)

    o_ref[...] = jnp.dot(h_scr[...], w_ref[...], preferred_element_type=F32).astype(o_ref.dtype)


def _mem_kv(mem2d, gain, w, *, tm=1024, tn=1024):
    t, d = mem2d.shape
    n = w.shape[1]
    assert t % tm == 0 and n % tn == 0
    return pl.pallas_call(
        _norm_matmul_kernel,
        grid=(t // tm, n // tn),
        in_specs=[
            pl.BlockSpec((tm, d), lambda i, j: (i, 0)),
            pl.BlockSpec((1, d), lambda i, j: (0, 0)),
            pl.BlockSpec((d, tn), lambda i, j: (0, j)),
        ],
        out_specs=pl.BlockSpec((tm, tn), lambda i, j: (i, j)),
        out_shape=jax.ShapeDtypeStruct((t, n), BF16),
        scratch_shapes=[pltpu.VMEM((tm, d), BF16)],
        compiler_params=_params("parallel", "arbitrary"),
        name="mem_kv",
    )(mem2d, gain, w)


def _attn_am_kernel(qa_ref, ka_ref, va_ref, qm_ref, kvm_ref, oa_ref, om_ref, va1_scr, vm1_scr):
    @pl.when(pl.program_id(1) == 0)
    def _():
        for kv in range(N_KV_A):
            va1_scr[kv, :, :HEAD_DIM] = va_ref[:, kv * HEAD_DIM:(kv + 1) * HEAD_DIM]
            va1_scr[kv, :, HEAD_DIM:] = jnp.ones((va_ref.shape[0], HEAD_DIM), BF16)
        for h in range(N_HEADS_M):
            vm1_scr[h, :, :HEAD_DIM] = kvm_ref[:, WIDTH_M + h * HEAD_DIM:WIDTH_M + (h + 1) * HEAD_DIM]
            vm1_scr[h, :, HEAD_DIM:] = jnp.ones((kvm_ref.shape[0], HEAD_DIM), BF16)

    def head_cols(h):
        return slice(h * HEAD_DIM, (h + 1) * HEAD_DIM)

    units = [(qa_ref, ka_ref, head_cols(h // GQA_GROUP), (va1_scr, h // GQA_GROUP), oa_ref, head_cols(h))
             for h in range(N_HEADS_A)]
    units += [(qm_ref, kvm_ref, head_cols(h), (vm1_scr, h), om_ref, head_cols(h)) for h in range(N_HEADS_M)]

    def scores(unit):
        q_ref, k_ref, kcols, _, _, cs = unit
        return _qk(q_ref[:, cs], k_ref[:, kcols])

    s_next = scores(units[0])
    for idx, unit in enumerate(units):
        s = s_next
        if idx + 1 < len(units):
            s_next = scores(units[idx + 1])
        _, _, _, (v1_scr, vi), o_ref, cs = unit
        m = jnp.max(s, axis=-1, keepdims=True)
        p = jnp.exp2((s - m) * (ATTN_SCALE * LOG2E)).astype(BF16)
        ol = jnp.dot(p, v1_scr[vi], preferred_element_type=F32)
        o_ref[:, cs] = (ol[:, :HEAD_DIM] / ol[:, HEAD_DIM:]).astype(o_ref.dtype)


def _attn_am(pa, kvm, *, tq=1024):
    b, s, _ = pa.shape
    n_mem = kvm.shape[1]
    assert s % tq == 0 and PA_KA % WIDTH_KV_A == 0 and PA_VA % WIDTH_KV_A == 0
    return pl.pallas_call(
        _attn_am_kernel,
        grid=(b, s // tq),
        in_specs=[
            pl.BlockSpec((None, tq, WIDTH_A), lambda bi, i: (bi, i, PA_QA // WIDTH_A)),
            pl.BlockSpec((None, s, WIDTH_KV_A), lambda bi, i: (bi, 0, PA_KA // WIDTH_KV_A)),
            pl.BlockSpec((None, s, WIDTH_KV_A), lambda bi, i: (bi, 0, PA_VA // WIDTH_KV_A)),
            pl.BlockSpec((None, tq, WIDTH_M), lambda bi, i: (bi, i, PA_QM // WIDTH_M)),
            pl.BlockSpec((None, n_mem, 2 * WIDTH_M), lambda bi, i: (bi, 0, 0)),
        ],
        out_specs=[pl.BlockSpec((None, tq, WIDTH_A), lambda bi, i: (bi, i, 0)),
                   pl.BlockSpec((None, tq, WIDTH_M), lambda bi, i: (bi, i, 0))],
        out_shape=[jax.ShapeDtypeStruct((b, s, WIDTH_A), BF16),
                   jax.ShapeDtypeStruct((b, s, WIDTH_M), BF16)],
        scratch_shapes=[pltpu.VMEM((N_KV_A, s, 2 * HEAD_DIM), BF16),
                        pltpu.VMEM((N_HEADS_M, n_mem, 2 * HEAD_DIM), BF16)],
        compiler_params=_params("parallel", "arbitrary"),
        name="attn_am",
    )(pa, pa, pa, pa, kvm)


B_TQ = 128
B_HALF = 64
B_SKEW = 2
assert all(w // (2 * d) == B_HALF for w, d in DIL_PAIRS)


def _band_window(sub_len):
    return min(sub_len, B_TQ + 2 * B_HALF)


def _band_blocks(sub_len):
    w = _band_window(sub_len)
    nblk = sub_len // B_TQ
    out = []
    for qb in range(nblk):
        a = qb * B_TQ
        ks = min(max(a - B_HALF, 0), sub_len - w)
        case = 0 if qb == 0 else (2 if qb == nblk - 1 else 1)
        out.append((a, ks, case))
    return out


def _t5_bucket(rel):
    nb = NUM_BUCKETS // 2
    max_exact = nb // 2
    base = jnp.where(rel > 0, nb, 0)
    n = jnp.abs(rel)
    nf = jnp.maximum(n, 1).astype(F32)
    large = max_exact + (jnp.log(nf / max_exact) / math.log(MAX_DISTANCE / max_exact)
                         * (nb - max_exact)).astype(jnp.int32)
    large = jnp.minimum(large, nb - 1)
    return base + jnp.where(n < max_exact, n, large)


def _band_buckets(sub_len, dil):
    w = _band_window(sub_len)
    offsets = {}
    for a, ks, case in _band_blocks(sub_len):
        assert offsets.setdefault(case, ks - a) == ks - a
    i = np.arange(B_TQ)[:, None]
    j = np.arange(w)[None, :]
    tiles = []
    for case in sorted(offsets):
        rel = jnp.asarray(offsets[case] + j - i, jnp.int32)
        tiles.append(jnp.where(jnp.abs(rel) <= B_HALF, _t5_bucket(rel * dil), -1))
    return jnp.stack(tiles)


def _band_bias_kernel(tab_ref, bucket_ref, o_ref, *, group):
    bucket = bucket_ref[...]
    for h in range(HEADS_PER_DIL):
        acc = jnp.full(bucket.shape, NEG_INF, F32)
        for kb in range(NUM_BUCKETS):
            acc = jnp.where(bucket == kb, tab_ref[kb, group * HEADS_PER_DIL + h] * LOG2E, acc)
        o_ref[:, h] = acc


def _band_bias(rel_bias, group, sub_len, dil):
    bucket = _band_buckets(sub_len, dil)
    ncase, tq, w = bucket.shape
    return pl.pallas_call(
        functools.partial(_band_bias_kernel, group=group),
        in_specs=[pl.BlockSpec(memory_space=pltpu.SMEM),
                  pl.BlockSpec(memory_space=pltpu.VMEM)],
        out_specs=pl.BlockSpec(memory_space=pltpu.VMEM),
        out_shape=jax.ShapeDtypeStruct((ncase, HEADS_PER_DIL, tq, w), F32),
        name=f"band_bias_dil{dil}",
    )(rel_bias.astype(F32), bucket)


def _attn_b_kernel(*refs, seq):
    qkv_refs, bias_refs = refs[:3 * N_DIL], refs[3 * N_DIL:4 * N_DIL]
    o_ref, o_scr, lse_scr = refs[4 * N_DIL:]
    tiles = [(g, dil, r, a, ks, case)
             for g, (_, dil) in enumerate(DIL_PAIRS)
             for r in range(dil)
             for a, ks, case in _band_blocks(seq // dil)]

    def rows(dil, r, start, n):
        return pl.ds(start, n) if dil == 1 else pl.ds(r + dil * start, n, stride=dil)

    def load_scores(tile):
        g, dil, r, a, ks, case = tile
        q_ref, k_ref, v_ref = qkv_refs[3 * g:3 * g + 3]
        w = _band_window(seq // dil)
        q = q_ref[rows(dil, r, a, B_TQ), :].astype(BF16)
        k = k_ref[rows(dil, r, ks, w), :].astype(BF16)
        v = v_ref[rows(dil, r, ks, w), :].astype(BF16)
        v1 = jnp.concatenate([v, jnp.ones_like(v)], axis=-1)
        return _qk(q, k) * (ATTN_SCALE * LOG2E) + bias_refs[g][case], v1

    def softmax_pv(s, v1):
        m = jnp.max(s, axis=-1, keepdims=True)
        p = jnp.exp2(s - m).astype(BF16)
        return jnp.dot(p, v1, preferred_element_type=F32), m

    def store(tile, ol, m):
        g, dil, r, a, _, _ = tile
        l = ol[:, HEAD_DIM:]
        o_scr[g, rows(dil, r, a, B_TQ), :] = ol[:, :HEAD_DIM] / l
        lse_scr[g, rows(dil, r, a, B_TQ), :] = m + jnp.log(l) * LOG2E

    n = len(tiles)
    scored, popped = {}, {}
    for step in range(n + 2 * B_SKEW):
        if step < n:
            scored[step] = load_scores(tiles[step])
        t = step - B_SKEW
        if 0 <= t < n:
            popped[t] = softmax_pv(*scored.pop(t))
        t = step - 2 * B_SKEW
        if 0 <= t < n:
            store(tiles[t], *popped.pop(t))
    lse = [lse_scr[g] for g in range(N_DIL)]
    mx = functools.reduce(jnp.maximum, lse)
    e = [jnp.exp2(x - mx) for x in lse]
    den = functools.reduce(lambda a, b: a + b, e)
    for g in range(N_DIL):
        o_ref[:, g * HEAD_DIM:(g + 1) * HEAD_DIM] = ((e[g] / den) * o_scr[g]).astype(o_ref.dtype)


def _attn_b(pa, pb, biases):
    b, s, _ = pa.shape

    def col(cols0, part, g):
        blk = (cols0 + (3 * g + part) * GROUP_W) // HEAD_DIM
        return pl.BlockSpec((None, s, HEAD_DIM), lambda bi, h: (bi, 0, blk + h))

    in_specs = [col(PA_B0, part, 0) for part in range(3)]
    operands = [pa] * 3
    for g in range(1, N_DIL):
        in_specs += [col(0, part, g - 1) for part in range(3)]
        operands += [pb] * 3
    for bias in biases:
        ncase, _, tq, w = bias.shape
        in_specs.append(pl.BlockSpec((ncase, None, tq, w), lambda bi, h: (0, h, 0, 0)))
    return pl.pallas_call(
        functools.partial(_attn_b_kernel, seq=s),
        grid=(b, HEADS_PER_DIL),
        in_specs=in_specs,
        out_specs=pl.BlockSpec((None, s, N_DIL * HEAD_DIM), lambda bi, h: (bi, 0, h)),
        out_shape=jax.ShapeDtypeStruct((b, s, WIDTH_B), BF16),
        scratch_shapes=[pltpu.VMEM((N_DIL, s, HEAD_DIM), F32), pltpu.VMEM((N_DIL, s, HEAD_DIM), F32)],
        compiler_params=_params("parallel", "parallel"),
        name="attn_b",
    )(*operands, *biases)


def _heads_major(v):
    rest = v.shape[1:]
    v = v.reshape(N_DIL, HEADS_PER_DIL, HEAD_DIM, *rest)
    return jnp.swapaxes(v, 0, 1).reshape(WIDTH_B, *rest)


def _out_proj_kernel(oa_ref, ob_ref, om_ref, x_ref, ga_ref, gb_ref, gm_ref, w_ref, gpost_ref, o_ref):
    mix = jnp.concatenate([_rms(oa_ref[...].astype(F32), ga_ref[...]).astype(BF16),
                           _rms(ob_ref[...].astype(F32), gb_ref[...]).astype(BF16),
                           _rms(om_ref[...].astype(F32), gm_ref[...]).astype(BF16)], axis=-1)
    y = jnp.dot(mix, w_ref[...], preferred_element_type=F32)
    o_ref[...] = x_ref[...] + _rms(y, gpost_ref[...])


def _out_proj(oa, ob, om, x2d, ga, gb, gm, w, gpost, *, tm=512):
    t, d = x2d.shape
    assert t % tm == 0

    def rows(width):
        return pl.BlockSpec((tm, width), lambda i: (i, 0))

    return pl.pallas_call(
        _out_proj_kernel,
        grid=(t // tm,),
        in_specs=[rows(WIDTH_A), rows(WIDTH_B), rows(WIDTH_M), rows(d),
                  _resident(ga.shape), _resident(gb.shape), _resident(gm.shape),
                  _resident(w.shape), _resident(gpost.shape)],
        out_specs=rows(d),
        out_shape=jax.ShapeDtypeStruct((t, d), F32),
        compiler_params=_params("parallel"),
        name="out_proj",
    )(oa, ob, om, x2d, ga, gb, gm, w, gpost)


def _ffn_kernel(x_ref, gpre_ref, wup_ref, wdn_ref, gpost_ref, o_ref, h_scr):
    j = pl.program_id(1)
    last = pl.num_programs(1) - 1
    tm = x_ref.shape[0]

    def step(first, final):
        for rs in (_row_halves(tm) if first or final else (slice(0, tm),)):
            if first:
                h = _rms(x_ref[rs, :], gpre_ref[...]).astype(BF16)
                h_scr[rs, :] = h
            else:
                h = h_scr[rs, :]
            u = jnp.dot(h, wup_ref[...], preferred_element_type=F32)
            u = jnp.square(jnp.maximum(u, 0.0)).astype(BF16)
            acc = jnp.dot(u, wdn_ref[...], preferred_element_type=F32)
            if not first:
                acc = o_ref[rs, :] + acc
            if final:
                acc = x_ref[rs, :] + _rms(acc, gpost_ref[...])
            o_ref[rs, :] = acc

    pl.when(j == 0)(functools.partial(step, True, False))
    pl.when((j > 0) & (j < last))(functools.partial(step, False, False))
    pl.when(j == last)(functools.partial(step, False, True))


def _ffn(x2d, gpre, w_up, w_down, gpost, *, tm=1024, tf=FFN_CHUNK):
    t, d = x2d.shape
    d_ff = w_up.shape[1]
    assert t % tm == 0 and d_ff % tf == 0 and d_ff // tf >= 2
    return pl.pallas_call(
        _ffn_kernel,
        grid=(t // tm, d_ff // tf),
        in_specs=[
            pl.BlockSpec((tm, d), lambda i, j: (i, 0)),
            _resident(gpre.shape),
            pl.BlockSpec((d, tf), lambda i, j: (0, j)),
            pl.BlockSpec((tf, d), lambda i, j: (j, 0)),
            _resident(gpost.shape),
        ],
        out_specs=pl.BlockSpec((tm, d), lambda i, j: (i, 0)),
        out_shape=jax.ShapeDtypeStruct((t, d), F32),
        scratch_shapes=[pltpu.VMEM((tm, d), BF16)],
        compiler_params=_params("parallel", "arbitrary"),
        name="ffn",
    )(x2d, gpre, w_up, w_down, gpost)


def _rope_tables(seq_len):
    rows = seq_len // GRID_W
    row = jnp.repeat(jnp.arange(rows), GRID_W).astype(F32)
    col = jnp.tile(jnp.arange(GRID_W), rows).astype(F32)
    inv = ROPE_THETA ** (-(2.0 * jnp.arange(ROPE_PAIRS, dtype=F32)) / ROPE_AXIS_DIM)
    ang_r = row[:, None] * inv
    ang_c = col[:, None] * inv
    cos = jnp.concatenate([jnp.cos(ang_r)] * 2 + [jnp.cos(ang_c)] * 2, axis=-1)
    sin = jnp.concatenate([-jnp.sin(ang_r), jnp.sin(ang_r), -jnp.sin(ang_c), jnp.sin(ang_c)], axis=-1)
    return cos, sin


def _row(v):
    return v.reshape(1, -1).astype(F32)


def _permute_w_in(w):
    splits = [int(c) for c in np.cumsum([WIDTH_A, WIDTH_KV_A, WIDTH_KV_A, WIDTH_B, WIDTH_B, WIDTH_B])]
    qa, ka, va, qb, kb, vb, qm = jnp.split(w, splits, axis=1)

    def grp(m, g):
        return m[:, g * GROUP_W:(g + 1) * GROUP_W]

    parts = [qa, ka, qm, va] + [grp(m, 0) for m in (qb, kb, vb)]
    for g in range(1, N_DIL):
        parts += [grp(m, g) for m in (qb, kb, vb)]
    return jnp.concatenate(parts, axis=1)


def _encoder_layer(x, mem, p, biases):
    b, s, d = x.shape
    x2d = x.reshape(b * s, d)
    cos_tab, sin_tab = _rope_tables(s)
    pa, pb = _in_proj(x2d, p["pre_mix_norm"], p["w_in"], p["head_gain"], cos_tab, sin_tab, seq=s)
    pa = pa.reshape(b, s, PA_COLS)
    pb = pb.reshape(b, s, PB_COLS)
    kv = _mem_kv(mem.reshape(-1, d), p["mem_norm"], p["w_mem_kv"]).reshape(b, mem.shape[1], -1)

    oa, om = _attn_am(pa, kv)
    ob = _attn_b(pa, pb, biases)

    x1 = _out_proj(oa.reshape(b * s, WIDTH_A), ob.reshape(b * s, WIDTH_B), om.reshape(b * s, WIDTH_M), x2d,
                   p["out_norm_a"], p["out_norm_b"], p["out_norm_m"], p["w_out"], p["post_mix_norm"])
    y = _ffn(x1, p["pre_ffn_norm"], p["w_up"], p["w_down"], p["post_ffn_norm"])
    return y.reshape(b, s, d)


def kernel(x_prompt, x_sample, mem_prompt, mem_sample, rel_bias, pre_mix_norm, w_in, q_norm_a, k_norm_a, mem_norm, w_mem_kv, out_norm_a, out_norm_b, out_norm_m, w_out, post_mix_norm, pre_ffn_norm, w_up, w_down, post_ffn_norm):
    depth = w_in.shape[0]
    assert x_prompt.shape[1] == x_sample.shape[1]
    seq = x_prompt.shape[1]
    biases = [_band_bias(rel_bias, g, seq // dil, dil) for g, (_, dil) in enumerate(DIL_PAIRS)]
    layers = []
    for l in range(depth):
        w_o = w_out[l]
        w_o = jnp.concatenate([w_o[:WIDTH_A], _heads_major(w_o[WIDTH_A:WIDTH_A + WIDTH_B]),
                               w_o[WIDTH_A + WIDTH_B:]], axis=0)
        layers.append({
            "pre_mix_norm": _row(pre_mix_norm[l]),
            "w_in": _permute_w_in(w_in[l]).astype(BF16),
            "head_gain": _row(jnp.concatenate([jnp.tile(q_norm_a[l], N_HEADS_A),
                                               jnp.tile(k_norm_a[l], N_KV_A)])),
            "mem_norm": _row(mem_norm[l]),
            "w_mem_kv": w_mem_kv[l].astype(BF16),
            "out_norm_a": _row(out_norm_a[l]),
            "out_norm_b": _row(_heads_major(out_norm_b[l])),
            "out_norm_m": _row(out_norm_m[l]),
            "w_out": w_o.astype(BF16),
            "post_mix_norm": _row(post_mix_norm[l]),
            "pre_ffn_norm": _row(pre_ffn_norm[l]),
            "w_up": w_up[l].astype(BF16),
            "w_down": w_down[l].astype(BF16),
            "post_ffn_norm": _row(post_ffn_norm[l]),
        })

    def trunk(x, mem):
        for p in layers:
            x = _encoder_layer(x, mem, p, biases)
        return x

    return (trunk(x_prompt, mem_prompt), trunk(x_sample, mem_sample))
```

```python
import functools
import math

import jax
import jax.numpy as jnp
import numpy as np
from jax import lax
from jax.experimental import pallas as pl
from jax.experimental.pallas import tpu as pltpu

F32 = jnp.float32
BF16 = jnp.bfloat16

HEAD_DIM = 128
N_HEADS_A = 6
N_KV_A = 2
GQA_GROUP = N_HEADS_A // N_KV_A
DIL_PAIRS = ((128, 1), (512, 4), (2048, 16))
N_DIL = len(DIL_PAIRS)
HEADS_PER_DIL = 2
N_HEADS_B = N_DIL * HEADS_PER_DIL
N_HEADS_M = 4
WIDTH_A = N_HEADS_A * HEAD_DIM
WIDTH_KV_A = N_KV_A * HEAD_DIM
WIDTH_B = N_HEADS_B * HEAD_DIM
WIDTH_M = N_HEADS_M * HEAD_DIM
GROUP_W = HEADS_PER_DIL * HEAD_DIM
GRID_W = 64
ROPE_THETA = 10000.0
ROPE_AXIS_DIM = HEAD_DIM // 2
ROPE_PAIRS = ROPE_AXIS_DIM // 2
NUM_BUCKETS = 32
MAX_DISTANCE = 1024
EPS = 1e-6
NEG_INF = -1e30
ATTN_SCALE = HEAD_DIM ** -0.5
LOG2E = math.log2(math.e)

PA_QA = 0
PA_KA = PA_QA + WIDTH_A
ROPE_COLS = PA_KA + WIDTH_KV_A
PA_QM = ROPE_COLS
PA_VA = PA_QM + WIDTH_M
PA_B0 = PA_VA + WIDTH_KV_A
PA_COLS = PA_B0 + 3 * GROUP_W
PB_COLS = (N_DIL - 1) * 3 * GROUP_W
assert PA_QA % (GQA_GROUP * HEAD_DIM) == 0 and PA_QM % WIDTH_M == 0
IN_COLS = PA_COLS + PB_COLS
IN_CHUNK = 512
FFN_CHUNK = 1024

BF16_SUBLANES = 16
V7X_VMEM_BYTES = 64 * 1024 * 1024
VMEM_LIMIT_BYTES = V7X_VMEM_BYTES - 4 * 1024 * 1024


def _params(*semantics):
    return pltpu.CompilerParams(dimension_semantics=semantics,
                                vmem_limit_bytes=VMEM_LIMIT_BYTES)


def _resident(shape):
    zeros = (0,) * len(shape)
    return pl.BlockSpec(shape, lambda *_: zeros, pipeline_mode=pl.Buffered(1))


def _rms(v, g):
    ms = jnp.mean(v * v, axis=-1, keepdims=True)
    return v * lax.rsqrt(ms + EPS) * g


def _row_halves(rows):
    return (slice(0, rows // 2), slice(rows // 2, rows))


def _qk(q, k):
    return lax.dot_general(q, k, (((1,), (1,)), ((), ())), preferred_element_type=F32)


def _cast_rider(weight, n_steps):
    rows, cols = weight.shape
    assert rows % (n_steps * BF16_SUBLANES) == 0
    spec = pl.BlockSpec((rows // n_steps, cols), lambda i: (i, 0))
    return spec, spec, jax.ShapeDtypeStruct((rows, cols), BF16)


def _in_proj_kernel(x_ref, g_ref, w_ref, hg_ref, cos_ref, sin_ref, *rest):
    if len(rest) == 4:
        cast_src_ref, pa_ref, pb_ref, cast_dst_ref = rest
        cast_dst_ref[...] = cast_src_ref[...].astype(BF16)
    else:
        pa_ref, pb_ref = rest
    h = _rms(x_ref[...], g_ref[...]).astype(BF16)
    tm = h.shape[0]
    cos = cos_ref[...]
    sin = sin_ref[...]
    lane = lax.broadcasted_iota(jnp.int32, (tm, HEAD_DIM), 1)
    low_half = (lane % ROPE_AXIS_DIM) < ROPE_PAIRS
    for c0 in range(0, IN_COLS, IN_CHUNK):
        acc = jnp.dot(h, w_ref[:, c0:c0 + IN_CHUNK], preferred_element_type=F32)
        if c0 < ROPE_COLS:
            for hh in range(IN_CHUNK // HEAD_DIM):
                cs = slice(c0 + hh * HEAD_DIM, c0 + (hh + 1) * HEAD_DIM)
                y = _rms(acc[:, hh * HEAD_DIM:(hh + 1) * HEAD_DIM], hg_ref[:, cs])
                partner = jnp.where(low_half, pltpu.roll(y, HEAD_DIM - ROPE_PAIRS, 1),
                                    pltpu.roll(y, ROPE_PAIRS, 1))
                pa_ref[:, cs] = (y * cos + partner * sin).astype(BF16)
        elif c0 < PA_COLS:
            pa_ref[:, c0:c0 + IN_CHUNK] = acc.astype(BF16)
        else:
            pb_ref[:, c0 - PA_COLS:c0 - PA_COLS + IN_CHUNK] = acc


def _in_proj(x2d, gain, w, head_gain, cos_tab, sin_tab, *, seq, tm=512, cast=None):
    t, d = x2d.shape
    assert t % tm == 0 and seq % tm == 0
    assert ROPE_COLS % IN_CHUNK == 0 and PA_COLS % IN_CHUNK == 0 and IN_COLS % IN_CHUNK == 0
    pos_blocks = seq // tm
    in_specs = [
        pl.BlockSpec((tm, d), lambda i: (i, 0)),
        _resident(gain.shape),
        _resident(w.shape),
        _resident(head_gain.shape),
        pl.BlockSpec((tm, HEAD_DIM), lambda i: (i % pos_blocks, 0)),
        pl.BlockSpec((tm, HEAD_DIM), lambda i: (i % pos_blocks, 0)),
    ]
    out_specs = [pl.BlockSpec((tm, PA_COLS), lambda i: (i, 0)),
                 pl.BlockSpec((tm, PB_COLS), lambda i: (i, 0))]
    out_shape = [jax.ShapeDtypeStruct((t, PA_COLS), BF16),
                 jax.ShapeDtypeStruct((t, PB_COLS), F32)]
    operands = [x2d, gain, w, head_gain, cos_tab, sin_tab]
    if cast is not None:
        src_spec, dst_spec, dst_shape = _cast_rider(cast, t // tm)
        in_specs.append(src_spec)
        out_specs.append(dst_spec)
        out_shape.append(dst_shape)
        operands.append(cast)
    return pl.pallas_call(
        _in_proj_kernel,
        grid=(t // tm,),
        in_specs=in_specs,
        out_specs=out_specs,
        out_shape=out_shape,
        compiler_params=_params("parallel"),
        name="in_proj",
    )(*operands)


def _norm_matmul_kernel(x_ref, g_ref, w_ref, o_ref, h_scr):
    @pl.when(pl.program_id(1) == 0)
    def _():
        h_scr[...] = _rms(x_ref[...], g_ref[...]).astype(BF16)

    o_ref[...] = jnp.dot(h_scr[...], w_ref[...], preferred_element_type=F32).astype(o_ref.dtype)


def _mem_kv(mem2d, gain, w, *, tm=1024, tn=1024):
    t, d = mem2d.shape
    n = w.shape[1]
    assert t % tm == 0 and n % tn == 0
    return pl.pallas_call(
        _norm_matmul_kernel,
        grid=(t // tm, n // tn),
        in_specs=[
            pl.BlockSpec((tm, d), lambda i, j: (i, 0)),
            pl.BlockSpec((1, d), lambda i, j: (0, 0)),
            pl.BlockSpec((d, tn), lambda i, j: (0, j)),
        ],
        out_specs=pl.BlockSpec((tm, tn), lambda i, j: (i, j)),
        out_shape=jax.ShapeDtypeStruct((t, n), BF16),
        scratch_shapes=[pltpu.VMEM((tm, d), BF16)],
        compiler_params=_params("parallel", "arbitrary"),
        name="mem_kv",
    )(mem2d, gain, w)


def _attn_am_kernel(qa_ref, ka_ref, va_ref, qm_ref, kvm_ref, oa_ref, om_ref, va1_scr, vm1_scr):
    @pl.when(pl.program_id(1) == 0)
    def _():
        for kv in range(N_KV_A):
            va1_scr[kv, :, :HEAD_DIM] = va_ref[:, kv * HEAD_DIM:(kv + 1) * HEAD_DIM]
            va1_scr[kv, :, HEAD_DIM:] = jnp.ones((va_ref.shape[0], HEAD_DIM), BF16)
        for h in range(N_HEADS_M):
            vm1_scr[h, :, :HEAD_DIM] = kvm_ref[:, WIDTH_M + h * HEAD_DIM:WIDTH_M + (h + 1) * HEAD_DIM]
            vm1_scr[h, :, HEAD_DIM:] = jnp.ones((kvm_ref.shape[0], HEAD_DIM), BF16)

    def head_cols(h):
        return slice(h * HEAD_DIM, (h + 1) * HEAD_DIM)

    units = [(qa_ref, ka_ref, head_cols(h // GQA_GROUP), (va1_scr, h // GQA_GROUP), oa_ref, head_cols(h))
             for h in range(N_HEADS_A)]
    units += [(qm_ref, kvm_ref, head_cols(h), (vm1_scr, h), om_ref, head_cols(h)) for h in range(N_HEADS_M)]

    def scores(unit):
        q_ref, k_ref, kcols, _, _, cs = unit
        return _qk(q_ref[:, cs], k_ref[:, kcols])

    s_next = scores(units[0])
    for idx, unit in enumerate(units):
        s = s_next
        if idx + 1 < len(units):
            s_next = scores(units[idx + 1])
        _, _, _, (v1_scr, vi), o_ref, cs = unit
        m = jnp.max(s, axis=-1, keepdims=True)
        p = jnp.exp2((s - m) * (ATTN_SCALE * LOG2E)).astype(BF16)
        ol = jnp.dot(p, v1_scr[vi], preferred_element_type=F32)
        o_ref[:, cs] = (ol[:, :HEAD_DIM] / ol[:, HEAD_DIM:]).astype(o_ref.dtype)


def _attn_am(pa, kvm, *, tq=1024):
    b, s, _ = pa.shape
    n_mem = kvm.shape[1]
    assert s % tq == 0 and PA_KA % WIDTH_KV_A == 0 and PA_VA % WIDTH_KV_A == 0
    return pl.pallas_call(
        _attn_am_kernel,
        grid=(b, s // tq),
        in_specs=[
            pl.BlockSpec((None, tq, WIDTH_A), lambda bi, i: (bi, i, PA_QA // WIDTH_A)),
            pl.BlockSpec((None, s, WIDTH_KV_A), lambda bi, i: (bi, 0, PA_KA // WIDTH_KV_A)),
            pl.BlockSpec((None, s, WIDTH_KV_A), lambda bi, i: (bi, 0, PA_VA // WIDTH_KV_A)),
            pl.BlockSpec((None, tq, WIDTH_M), lambda bi, i: (bi, i, PA_QM // WIDTH_M)),
            pl.BlockSpec((None, n_mem, 2 * WIDTH_M), lambda bi, i: (bi, 0, 0)),
        ],
        out_specs=[pl.BlockSpec((None, tq, WIDTH_A), lambda bi, i: (bi, i, 0)),
                   pl.BlockSpec((None, tq, WIDTH_M), lambda bi, i: (bi, i, 0))],
        out_shape=[jax.ShapeDtypeStruct((b, s, WIDTH_A), BF16),
                   jax.ShapeDtypeStruct((b, s, WIDTH_M), BF16)],
        scratch_shapes=[pltpu.VMEM((N_KV_A, s, 2 * HEAD_DIM), BF16),
                        pltpu.VMEM((N_HEADS_M, n_mem, 2 * HEAD_DIM), BF16)],
        compiler_params=_params("parallel", "arbitrary"),
        name="attn_am",
    )(pa, pa, pa, pa, kvm)


B_TQ = 128
B_HALF = 64
B_SKEW = 2
assert all(w // (2 * d) == B_HALF for w, d in DIL_PAIRS)


def _band_window(sub_len):
    return min(sub_len, B_TQ + 2 * B_HALF)


def _band_blocks(sub_len):
    w = _band_window(sub_len)
    nblk = sub_len // B_TQ
    out = []
    for qb in range(nblk):
        a = qb * B_TQ
        ks = min(max(a - B_HALF, 0), sub_len - w)
        case = 0 if qb == 0 else (2 if qb == nblk - 1 else 1)
        out.append((a, ks, case))
    return out


def _t5_bucket(rel):
    nb = NUM_BUCKETS // 2
    max_exact = nb // 2
    base = jnp.where(rel > 0, nb, 0)
    n = jnp.abs(rel)
    nf = jnp.maximum(n, 1).astype(F32)
    large = max_exact + (jnp.log(nf / max_exact) / math.log(MAX_DISTANCE / max_exact)
                         * (nb - max_exact)).astype(jnp.int32)
    large = jnp.minimum(large, nb - 1)
    return base + jnp.where(n < max_exact, n, large)


def _band_buckets(sub_len, dil):
    w = _band_window(sub_len)
    offsets = {}
    for a, ks, case in _band_blocks(sub_len):
        assert offsets.setdefault(case, ks - a) == ks - a
    i = np.arange(B_TQ)[:, None]
    j = np.arange(w)[None, :]
    tiles = []
    for case in sorted(offsets):
        rel = jnp.asarray(offsets[case] + j - i, jnp.int32)
        tiles.append(jnp.where(jnp.abs(rel) <= B_HALF, _t5_bucket(rel * dil), -1))
    return jnp.stack(tiles)


def _band_bias_kernel(tab_ref, bucket_ref, o_ref, *, group):
    bucket = bucket_ref[...]
    for h in range(HEADS_PER_DIL):
        acc = jnp.full(bucket.shape, NEG_INF, F32)
        for kb in range(NUM_BUCKETS):
            acc = jnp.where(bucket == kb, tab_ref[kb, group * HEADS_PER_DIL + h] * LOG2E, acc)
        o_ref[:, h] = acc


def _band_bias(rel_bias, group, sub_len, dil):
    bucket = _band_buckets(sub_len, dil)
    ncase, tq, w = bucket.shape
    return pl.pallas_call(
        functools.partial(_band_bias_kernel, group=group),
        in_specs=[pl.BlockSpec(memory_space=pltpu.SMEM),
                  pl.BlockSpec(memory_space=pltpu.VMEM)],
        out_specs=pl.BlockSpec(memory_space=pltpu.VMEM),
        out_shape=jax.ShapeDtypeStruct((ncase, HEADS_PER_DIL, tq, w), F32),
        name=f"band_bias_dil{dil}",
    )(rel_bias.astype(F32), bucket)


def _attn_b_kernel(*refs, seq):
    qkv_refs, bias_refs = refs[:3 * N_DIL], refs[3 * N_DIL:4 * N_DIL]
    o_ref, o_scr, lse_scr = refs[4 * N_DIL:]
    tiles = [(g, dil, r, a, ks, case)
             for g, (_, dil) in enumerate(DIL_PAIRS)
             for r in range(dil)
             for a, ks, case in _band_blocks(seq // dil)]

    def rows(dil, r, start, n):
        return pl.ds(start, n) if dil == 1 else pl.ds(r + dil * start, n, stride=dil)

    def load_scores(tile):
        g, dil, r, a, ks, case = tile
        q_ref, k_ref, v_ref = qkv_refs[3 * g:3 * g + 3]
        w = _band_window(seq // dil)
        q = q_ref[rows(dil, r, a, B_TQ), :].astype(BF16)
        k = k_ref[rows(dil, r, ks, w), :].astype(BF16)
        v = v_ref[rows(dil, r, ks, w), :].astype(BF16)
        v1 = jnp.concatenate([v, jnp.ones_like(v)], axis=-1)
        return _qk(q, k) * (ATTN_SCALE * LOG2E) + bias_refs[g][case], v1

    def softmax_pv(s, v1):
        m = jnp.max(s, axis=-1, keepdims=True)
        p = jnp.exp2(s - m).astype(BF16)
        return jnp.dot(p, v1, preferred_element_type=F32), m

    def store(tile, ol, m):
        g, dil, r, a, _, _ = tile
        l = ol[:, HEAD_DIM:]
        o_scr[g, rows(dil, r, a, B_TQ), :] = ol[:, :HEAD_DIM] / l
        lse_scr[g, rows(dil, r, a, B_TQ), :] = m + jnp.log(l) * LOG2E

    n = len(tiles)
    scored, popped = {}, {}
    for step in range(n + 2 * B_SKEW):
        if step < n:
            scored[step] = load_scores(tiles[step])
        t = step - B_SKEW
        if 0 <= t < n:
            popped[t] = softmax_pv(*scored.pop(t))
        t = step - 2 * B_SKEW
        if 0 <= t < n:
            store(tiles[t], *popped.pop(t))
    lse = [lse_scr[g] for g in range(N_DIL)]
    mx = functools.reduce(jnp.maximum, lse)
    e = [jnp.exp2(x - mx) for x in lse]
    den = functools.reduce(lambda a, b: a + b, e)
    for g in range(N_DIL):
        o_ref[:, g * HEAD_DIM:(g + 1) * HEAD_DIM] = ((e[g] / den) * o_scr[g]).astype(o_ref.dtype)


def _attn_b(pa, pb, biases):
    b, s, _ = pa.shape

    def col(cols0, part, g):
        blk = (cols0 + (3 * g + part) * GROUP_W) // HEAD_DIM
        return pl.BlockSpec((None, s, HEAD_DIM), lambda bi, h: (bi, 0, blk + h))

    in_specs = [col(PA_B0, part, 0) for part in range(3)]
    operands = [pa] * 3
    for g in range(1, N_DIL):
        in_specs += [col(0, part, g - 1) for part in range(3)]
        operands += [pb] * 3
    for bias in biases:
        ncase, _, tq, w = bias.shape
        in_specs.append(pl.BlockSpec((ncase, None, tq, w), lambda bi, h: (0, h, 0, 0)))
    return pl.pallas_call(
        functools.partial(_attn_b_kernel, seq=s),
        grid=(b, HEADS_PER_DIL),
        in_specs=in_specs,
        out_specs=pl.BlockSpec((None, s, N_DIL * HEAD_DIM), lambda bi, h: (bi, 0, h)),
        out_shape=jax.ShapeDtypeStruct((b, s, WIDTH_B), BF16),
        scratch_shapes=[pltpu.VMEM((N_DIL, s, HEAD_DIM), F32), pltpu.VMEM((N_DIL, s, HEAD_DIM), F32)],
        compiler_params=_params("parallel", "parallel"),
        name="attn_b",
    )(*operands, *biases)


def _heads_major(v):
    rest = v.shape[1:]
    v = v.reshape(N_DIL, HEADS_PER_DIL, HEAD_DIM, *rest)
    return jnp.swapaxes(v, 0, 1).reshape(WIDTH_B, *rest)


def _out_proj_kernel(oa_ref, ob_ref, om_ref, x_ref, ga_ref, gb_ref, gm_ref, w_ref, gpost_ref, *rest):
    if len(rest) == 3:
        cast_src_ref, o_ref, cast_dst_ref = rest
        cast_dst_ref[...] = cast_src_ref[...].astype(BF16)
    else:
        o_ref, = rest
    mix = jnp.concatenate([_rms(oa_ref[...].astype(F32), ga_ref[...]).astype(BF16),
                           _rms(ob_ref[...].astype(F32), gb_ref[...]).astype(BF16),
                           _rms(om_ref[...].astype(F32), gm_ref[...]).astype(BF16)], axis=-1)
    y = jnp.dot(mix, w_ref[...], preferred_element_type=F32)
    o_ref[...] = x_ref[...] + _rms(y, gpost_ref[...])


def _out_proj(oa, ob, om, x2d, ga, gb, gm, w, gpost, *, tm=512, cast=None):
    t, d = x2d.shape
    assert t % tm == 0

    def rows(width):
        return pl.BlockSpec((tm, width), lambda i: (i, 0))

    in_specs = [rows(WIDTH_A), rows(WIDTH_B), rows(WIDTH_M), rows(d),
                _resident(ga.shape), _resident(gb.shape), _resident(gm.shape),
                _resident(w.shape), _resident(gpost.shape)]
    out_specs = [rows(d)]
    out_shape = [jax.ShapeDtypeStruct((t, d), F32)]
    operands = [oa, ob, om, x2d, ga, gb, gm, w, gpost]
    if cast is not None:
        src_spec, dst_spec, dst_shape = _cast_rider(cast, t // tm)
        in_specs.append(src_spec)
        out_specs.append(dst_spec)
        out_shape.append(dst_shape)
        operands.append(cast)
    out = pl.pallas_call(
        _out_proj_kernel,
        grid=(t // tm,),
        in_specs=in_specs,
        out_specs=out_specs,
        out_shape=out_shape,
        compiler_params=_params("parallel"),
        name="out_proj",
    )(*operands)
    return out[0] if cast is None else tuple(out)


def _ffn_kernel(x_ref, gpre_ref, wup_ref, wdn_ref, gpost_ref, o_ref, h_scr):
    j = pl.program_id(1)
    last = pl.num_programs(1) - 1
    tm = x_ref.shape[0]

    def step(first, final):
        for rs in (_row_halves(tm) if first or final else (slice(0, tm),)):
            if first:
                h = _rms(x_ref[rs, :], gpre_ref[...]).astype(BF16)
                h_scr[rs, :] = h
            else:
                h = h_scr[rs, :]
            u = jnp.dot(h, wup_ref[...], preferred_element_type=F32)
            u = jnp.square(jnp.maximum(u, 0.0)).astype(BF16)
            acc = jnp.dot(u, wdn_ref[...], preferred_element_type=F32)
            if not first:
                acc = o_ref[rs, :] + acc
            if final:
                acc = x_ref[rs, :] + _rms(acc, gpost_ref[...])
            o_ref[rs, :] = acc

    pl.when(j == 0)(functools.partial(step, True, False))
    pl.when((j > 0) & (j < last))(functools.partial(step, False, False))
    pl.when(j == last)(functools.partial(step, False, True))


def _ffn(x2d, gpre, w_up, w_down, gpost, *, tm=1024, tf=FFN_CHUNK):
    t, d = x2d.shape
    d_ff = w_up.shape[1]
    assert t % tm == 0 and d_ff % tf == 0 and d_ff // tf >= 2
    return pl.pallas_call(
        _ffn_kernel,
        grid=(t // tm, d_ff // tf),
        in_specs=[
            pl.BlockSpec((tm, d), lambda i, j: (i, 0)),
            _resident(gpre.shape),
            pl.BlockSpec((d, tf), lambda i, j: (0, j)),
            pl.BlockSpec((tf, d), lambda i, j: (j, 0)),
            _resident(gpost.shape),
        ],
        out_specs=pl.BlockSpec((tm, d), lambda i, j: (i, 0)),
        out_shape=jax.ShapeDtypeStruct((t, d), F32),
        scratch_shapes=[pltpu.VMEM((tm, d), BF16)],
        compiler_params=_params("parallel", "arbitrary"),
        name="ffn",
    )(x2d, gpre, w_up, w_down, gpost)


def _rope_tables(seq_len):
    rows = seq_len // GRID_W
    row = jnp.repeat(jnp.arange(rows), GRID_W).astype(F32)
    col = jnp.tile(jnp.arange(GRID_W), rows).astype(F32)
    inv = ROPE_THETA ** (-(2.0 * jnp.arange(ROPE_PAIRS, dtype=F32)) / ROPE_AXIS_DIM)
    ang_r = row[:, None] * inv
    ang_c = col[:, None] * inv
    cos = jnp.concatenate([jnp.cos(ang_r)] * 2 + [jnp.cos(ang_c)] * 2, axis=-1)
    sin = jnp.concatenate([-jnp.sin(ang_r), jnp.sin(ang_r), -jnp.sin(ang_c), jnp.sin(ang_c)], axis=-1)
    return cos, sin


def _row(v):
    return v.reshape(1, -1).astype(F32)


def _permute_w_in(w):
    splits = [int(c) for c in np.cumsum([WIDTH_A, WIDTH_KV_A, WIDTH_KV_A, WIDTH_B, WIDTH_B, WIDTH_B])]
    qa, ka, va, qb, kb, vb, qm = jnp.split(w, splits, axis=1)

    def grp(m, g):
        return m[:, g * GROUP_W:(g + 1) * GROUP_W]

    parts = [qa, ka, qm, va] + [grp(m, 0) for m in (qb, kb, vb)]
    for g in range(1, N_DIL):
        parts += [grp(m, g) for m in (qb, kb, vb)]
    return jnp.concatenate(parts, axis=1)


def _encoder_layer(x, mem, p, biases, ffn_w):
    b, s, d = x.shape
    x2d = x.reshape(b * s, d)
    cast = ffn_w["w_up"].dtype != BF16
    cos_tab, sin_tab = _rope_tables(s)
    proj = _in_proj(x2d, p["pre_mix_norm"], p["w_in"], p["head_gain"], cos_tab, sin_tab, seq=s,
                    cast=ffn_w["w_up"] if cast else None)
    pa = proj[0].reshape(b, s, PA_COLS)
    pb = proj[1].reshape(b, s, PB_COLS)
    kv = _mem_kv(mem.reshape(-1, d), p["mem_norm"], p["w_mem_kv"]).reshape(b, mem.shape[1], -1)

    oa, om = _attn_am(pa, kv)
    ob = _attn_b(pa, pb, biases)

    x1 = _out_proj(oa.reshape(b * s, WIDTH_A), ob.reshape(b * s, WIDTH_B), om.reshape(b * s, WIDTH_M), x2d,
                   p["out_norm_a"], p["out_norm_b"], p["out_norm_m"], p["w_out"], p["post_mix_norm"],
                   cast=ffn_w["w_down"] if cast else None)
    if cast:
        x1, w_down = x1
        ffn_w = {"w_up": proj[2], "w_down": w_down}
    y = _ffn(x1, p["pre_ffn_norm"], ffn_w["w_up"], ffn_w["w_down"], p["post_ffn_norm"])
    return y.reshape(b, s, d), ffn_w


def kernel(x_prompt, x_sample, mem_prompt, mem_sample, rel_bias, pre_mix_norm, w_in, q_norm_a, k_norm_a, mem_norm, w_mem_kv, out_norm_a, out_norm_b, out_norm_m, w_out, post_mix_norm, pre_ffn_norm, w_up, w_down, post_ffn_norm):
    depth = w_in.shape[0]
    assert x_prompt.shape[1] == x_sample.shape[1]
    seq = x_prompt.shape[1]
    biases = [_band_bias(rel_bias, g, seq // dil, dil) for g, (_, dil) in enumerate(DIL_PAIRS)]
    layers = []
    for l in range(depth):
        w_o = w_out[l]
        w_o = jnp.concatenate([w_o[:WIDTH_A], _heads_major(w_o[WIDTH_A:WIDTH_A + WIDTH_B]),
                               w_o[WIDTH_A + WIDTH_B:]], axis=0)
        layers.append({
            "pre_mix_norm": _row(pre_mix_norm[l]),
            "w_in": _permute_w_in(w_in[l]).astype(BF16),
            "head_gain": _row(jnp.concatenate([jnp.tile(q_norm_a[l], N_HEADS_A),
                                               jnp.tile(k_norm_a[l], N_KV_A)])),
            "mem_norm": _row(mem_norm[l]),
            "w_mem_kv": w_mem_kv[l].astype(BF16),
            "out_norm_a": _row(out_norm_a[l]),
            "out_norm_b": _row(_heads_major(out_norm_b[l])),
            "out_norm_m": _row(out_norm_m[l]),
            "w_out": w_o.astype(BF16),
            "post_mix_norm": _row(post_mix_norm[l]),
            "pre_ffn_norm": _row(pre_ffn_norm[l]),
            "post_ffn_norm": _row(post_ffn_norm[l]),
        })

    xp, xs = x_prompt, x_sample
    for l, p in enumerate(layers):
        xp, ffn_w = _encoder_layer(xp, mem_prompt, p, biases, {"w_up": w_up[l], "w_down": w_down[l]})
        xs, _ = _encoder_layer(xs, mem_sample, p, biases, ffn_w)
    return (xp, xs)
```

```python
import functools
import math

import jax
import jax.numpy as jnp
import numpy as np
from jax import lax
from jax.experimental import pallas as pl
from jax.experimental.pallas import tpu as pltpu

F32 = jnp.float32
BF16 = jnp.bfloat16

HEAD_DIM = 128
N_HEADS_A = 6
N_KV_A = 2
GQA_GROUP = N_HEADS_A // N_KV_A
DIL_PAIRS = ((128, 1), (512, 4), (2048, 16))
N_DIL = len(DIL_PAIRS)
HEADS_PER_DIL = 2
N_HEADS_B = N_DIL * HEADS_PER_DIL
N_HEADS_M = 4
WIDTH_A = N_HEADS_A * HEAD_DIM
WIDTH_KV_A = N_KV_A * HEAD_DIM
WIDTH_B = N_HEADS_B * HEAD_DIM
WIDTH_M = N_HEADS_M * HEAD_DIM
GROUP_W = HEADS_PER_DIL * HEAD_DIM
GRID_W = 64
ROPE_THETA = 10000.0
ROPE_AXIS_DIM = HEAD_DIM // 2
ROPE_PAIRS = ROPE_AXIS_DIM // 2
NUM_BUCKETS = 32
MAX_DISTANCE = 1024
EPS = 1e-6
NEG_INF = -1e30
ATTN_SCALE = HEAD_DIM ** -0.5
LOG2E = math.log2(math.e)

PA_QA = 0
PA_KA = PA_QA + WIDTH_A
ROPE_COLS = PA_KA + WIDTH_KV_A
PA_QM = ROPE_COLS
PA_VA = PA_QM + WIDTH_M
PA_B0 = PA_VA + WIDTH_KV_A
PA_COLS = PA_B0 + 3 * GROUP_W
PB_COLS = (N_DIL - 1) * 3 * GROUP_W
assert PA_QA % (GQA_GROUP * HEAD_DIM) == 0 and PA_QM % WIDTH_M == 0
IN_COLS = PA_COLS + PB_COLS
IN_CHUNK = 512
FFN_CHUNK = 1024

BF16_SUBLANES = 16
V7X_VMEM_BYTES = 64 * 1024 * 1024
VMEM_LIMIT_BYTES = V7X_VMEM_BYTES - 4 * 1024 * 1024


def _params(*semantics):
    return pltpu.CompilerParams(dimension_semantics=semantics,
                                vmem_limit_bytes=VMEM_LIMIT_BYTES)


def _resident(shape):
    zeros = (0,) * len(shape)
    return pl.BlockSpec(shape, lambda *_: zeros, pipeline_mode=pl.Buffered(1))


def _rms(v, g):
    ms = jnp.mean(v * v, axis=-1, keepdims=True)
    return v * lax.rsqrt(ms + EPS) * g


def _row_halves(rows):
    return (slice(0, rows // 2), slice(rows // 2, rows))


def _qk(q, k):
    return lax.dot_general(q, k, (((1,), (1,)), ((), ())), preferred_element_type=F32)


def _cast_rider(weight, grid):
    rows, cols = weight.shape
    n_steps = math.prod(grid)
    assert rows % (n_steps * BF16_SUBLANES) == 0

    def slab(*idx):
        flat = 0
        for i, n in zip(idx, grid):
            flat = flat * n + i
        return (flat, 0)

    spec = pl.BlockSpec((rows // n_steps, cols), slab)
    return spec, spec, jax.ShapeDtypeStruct((rows, cols), BF16)


def _in_column_runs():
    starts = np.cumsum([0, WIDTH_A, WIDTH_KV_A, WIDTH_KV_A, WIDTH_B, WIDTH_B, WIDTH_B])
    qa, ka, va, qb, kb, vb, qm = (int(c) for c in starts)
    segs = [(qa, WIDTH_A), (ka, WIDTH_KV_A), (qm, WIDTH_M), (va, WIDTH_KV_A)]
    for g in range(N_DIL):
        segs += [(part + g * GROUP_W, GROUP_W) for part in (qb, kb, vb)]
    cols = np.concatenate([np.arange(s, s + w) for s, w in segs])
    assert cols.size == IN_COLS and np.array_equal(np.sort(cols), np.arange(IN_COLS))
    chunks = []
    for c0 in range(0, IN_COLS, IN_CHUNK):
        piece = cols[c0:c0 + IN_CHUNK]
        breaks = [0] + [int(i) + 1 for i in np.nonzero(np.diff(piece) != 1)[0]] + [IN_CHUNK]
        chunks.append([(int(piece[a]), b - a) for a, b in zip(breaks[:-1], breaks[1:])])
    return chunks


def _in_proj_kernel(x_ref, g_ref, w_ref, hg_ref, cos_ref, sin_ref, *rest):
    if len(rest) == 4:
        cast_src_ref, pa_ref, pb_ref, cast_dst_ref = rest
        cast_dst_ref[...] = cast_src_ref[...].astype(BF16)
    else:
        pa_ref, pb_ref = rest
    h = _rms(x_ref[...], g_ref[...]).astype(BF16)
    tm = h.shape[0]
    cos = cos_ref[...]
    sin = sin_ref[...]
    lane = lax.broadcasted_iota(jnp.int32, (tm, HEAD_DIM), 1)
    low_half = (lane % ROPE_AXIS_DIM) < ROPE_PAIRS
    for chunk, runs in enumerate(_in_column_runs()):
        c0 = chunk * IN_CHUNK
        parts = [jnp.dot(h, w_ref[:, s:s + n], preferred_element_type=F32) for s, n in runs]
        acc = parts[0] if len(parts) == 1 else jnp.concatenate(parts, axis=-1)
        if c0 < ROPE_COLS:
            for hh in range(IN_CHUNK // HEAD_DIM):
                cs = slice(c0 + hh * HEAD_DIM, c0 + (hh + 1) * HEAD_DIM)
                y = _rms(acc[:, hh * HEAD_DIM:(hh + 1) * HEAD_DIM], hg_ref[:, cs])
                partner = jnp.where(low_half, pltpu.roll(y, HEAD_DIM - ROPE_PAIRS, 1),
                                    pltpu.roll(y, ROPE_PAIRS, 1))
                pa_ref[:, cs] = (y * cos + partner * sin).astype(BF16)
        elif c0 < PA_COLS:
            pa_ref[:, c0:c0 + IN_CHUNK] = acc.astype(BF16)
        else:
            pb_ref[:, c0 - PA_COLS:c0 - PA_COLS + IN_CHUNK] = acc


def _in_proj(x2d, gain, w, head_gain, cos_tab, sin_tab, *, seq, tm=512, cast=None):
    t, d = x2d.shape
    assert t % tm == 0 and seq % tm == 0
    assert ROPE_COLS % IN_CHUNK == 0 and PA_COLS % IN_CHUNK == 0 and IN_COLS % IN_CHUNK == 0
    pos_blocks = seq // tm
    in_specs = [
        pl.BlockSpec((tm, d), lambda i: (i, 0)),
        _resident(gain.shape),
        _resident(w.shape),
        _resident(head_gain.shape),
        pl.BlockSpec((tm, HEAD_DIM), lambda i: (i % pos_blocks, 0)),
        pl.BlockSpec((tm, HEAD_DIM), lambda i: (i % pos_blocks, 0)),
    ]
    out_specs = [pl.BlockSpec((tm, PA_COLS), lambda i: (i, 0)),
                 pl.BlockSpec((tm, PB_COLS), lambda i: (i, 0))]
    out_shape = [jax.ShapeDtypeStruct((t, PA_COLS), BF16),
                 jax.ShapeDtypeStruct((t, PB_COLS), F32)]
    operands = [x2d, gain, w, head_gain, cos_tab, sin_tab]
    if cast is not None:
        src_spec, dst_spec, dst_shape = _cast_rider(cast, (t // tm,))
        in_specs.append(src_spec)
        out_specs.append(dst_spec)
        out_shape.append(dst_shape)
        operands.append(cast)
    return pl.pallas_call(
        _in_proj_kernel,
        grid=(t // tm,),
        in_specs=in_specs,
        out_specs=out_specs,
        out_shape=out_shape,
        compiler_params=_params("parallel"),
        name="in_proj",
    )(*operands)


def _norm_matmul_kernel(x_ref, g_ref, w_ref, o_ref, h_scr):
    @pl.when(pl.program_id(1) == 0)
    def _():
        h_scr[...] = _rms(x_ref[...], g_ref[...]).astype(BF16)

    o_ref[...] = jnp.dot(h_scr[...], w_ref[...], preferred_element_type=F32).astype(o_ref.dtype)


def _mem_kv(mem2d, gain, w, *, tm=1024, tn=1024):
    t, d = mem2d.shape
    n = w.shape[1]
    assert t % tm == 0 and n % tn == 0
    return pl.pallas_call(
        _norm_matmul_kernel,
        grid=(t // tm, n // tn),
        in_specs=[
            pl.BlockSpec((tm, d), lambda i, j: (i, 0)),
            pl.BlockSpec((1, d), lambda i, j: (0, 0)),
            pl.BlockSpec((d, tn), lambda i, j: (0, j)),
        ],
        out_specs=pl.BlockSpec((tm, tn), lambda i, j: (i, j)),
        out_shape=jax.ShapeDtypeStruct((t, n), BF16),
        scratch_shapes=[pltpu.VMEM((tm, d), BF16)],
        compiler_params=_params("parallel", "arbitrary"),
        name="mem_kv",
    )(mem2d, gain, w)


def _attn_am_kernel(qa_ref, ka_ref, va_ref, qm_ref, kvm_ref, *rest):
    if len(rest) == 6:
        cast_src_ref, oa_ref, om_ref, cast_dst_ref, va1_scr, vm1_scr = rest
        cast_dst_ref[...] = cast_src_ref[...].astype(BF16)
    else:
        oa_ref, om_ref, va1_scr, vm1_scr = rest

    @pl.when(pl.program_id(1) == 0)
    def _():
        for kv in range(N_KV_A):
            va1_scr[kv, :, :HEAD_DIM] = va_ref[:, kv * HEAD_DIM:(kv + 1) * HEAD_DIM]
            va1_scr[kv, :, HEAD_DIM:] = jnp.ones((va_ref.shape[0], HEAD_DIM), BF16)
        for h in range(N_HEADS_M):
            vm1_scr[h, :, :HEAD_DIM] = kvm_ref[:, WIDTH_M + h * HEAD_DIM:WIDTH_M + (h + 1) * HEAD_DIM]
            vm1_scr[h, :, HEAD_DIM:] = jnp.ones((kvm_ref.shape[0], HEAD_DIM), BF16)

    def head_cols(h):
        return slice(h * HEAD_DIM, (h + 1) * HEAD_DIM)

    units = [(qa_ref, ka_ref, head_cols(h // GQA_GROUP), (va1_scr, h // GQA_GROUP), oa_ref, head_cols(h))
             for h in range(N_HEADS_A)]
    units += [(qm_ref, kvm_ref, head_cols(h), (vm1_scr, h), om_ref, head_cols(h)) for h in range(N_HEADS_M)]

    def scores(unit):
        q_ref, k_ref, kcols, _, _, cs = unit
        return _qk(q_ref[:, cs], k_ref[:, kcols])

    s_next = scores(units[0])
    for idx, unit in enumerate(units):
        s = s_next
        if idx + 1 < len(units):
            s_next = scores(units[idx + 1])
        _, _, _, (v1_scr, vi), o_ref, cs = unit
        m = jnp.max(s, axis=-1, keepdims=True)
        p = jnp.exp2((s - m) * (ATTN_SCALE * LOG2E)).astype(BF16)
        ol = jnp.dot(p, v1_scr[vi], preferred_element_type=F32)
        o_ref[:, cs] = (ol[:, :HEAD_DIM] / ol[:, HEAD_DIM:]).astype(o_ref.dtype)


def _attn_am(pa, kvm, *, tq=1024, cast=None):
    b, s, _ = pa.shape
    n_mem = kvm.shape[1]
    assert s % tq == 0 and PA_KA % WIDTH_KV_A == 0 and PA_VA % WIDTH_KV_A == 0
    grid = (b, s // tq)
    in_specs = [
        pl.BlockSpec((None, tq, WIDTH_A), lambda bi, i: (bi, i, PA_QA // WIDTH_A)),
        pl.BlockSpec((None, s, WIDTH_KV_A), lambda bi, i: (bi, 0, PA_KA // WIDTH_KV_A)),
        pl.BlockSpec((None, s, WIDTH_KV_A), lambda bi, i: (bi, 0, PA_VA // WIDTH_KV_A)),
        pl.BlockSpec((None, tq, WIDTH_M), lambda bi, i: (bi, i, PA_QM // WIDTH_M)),
        pl.BlockSpec((None, n_mem, 2 * WIDTH_M), lambda bi, i: (bi, 0, 0)),
    ]
    out_specs = [pl.BlockSpec((None, tq, WIDTH_A), lambda bi, i: (bi, i, 0)),
                 pl.BlockSpec((None, tq, WIDTH_M), lambda bi, i: (bi, i, 0))]
    out_shape = [jax.ShapeDtypeStruct((b, s, WIDTH_A), BF16),
                 jax.ShapeDtypeStruct((b, s, WIDTH_M), BF16)]
    operands = [pa, pa, pa, pa, kvm]
    if cast is not None:
        src_spec, dst_spec, dst_shape = _cast_rider(cast, grid)
        in_specs.append(src_spec)
        out_specs.append(dst_spec)
        out_shape.append(dst_shape)
        operands.append(cast)
    return pl.pallas_call(
        _attn_am_kernel,
        grid=grid,
        in_specs=in_specs,
        out_specs=out_specs,
        out_shape=out_shape,
        scratch_shapes=[pltpu.VMEM((N_KV_A, s, 2 * HEAD_DIM), BF16),
                        pltpu.VMEM((N_HEADS_M, n_mem, 2 * HEAD_DIM), BF16)],
        compiler_params=_params("parallel", "arbitrary"),
        name="attn_am",
    )(*operands)


B_TQ = 128
B_HALF = 64
B_SKEW = 2
assert all(w // (2 * d) == B_HALF for w, d in DIL_PAIRS)


def _band_window(sub_len):
    return min(sub_len, B_TQ + 2 * B_HALF)


def _band_blocks(sub_len):
    w = _band_window(sub_len)
    nblk = sub_len // B_TQ
    out = []
    for qb in range(nblk):
        a = qb * B_TQ
        ks = min(max(a - B_HALF, 0), sub_len - w)
        case = 0 if qb == 0 else (2 if qb == nblk - 1 else 1)
        out.append((a, ks, case))
    return out


def _t5_bucket(rel):
    nb = NUM_BUCKETS // 2
    max_exact = nb // 2
    base = np.where(rel > 0, nb, 0)
    n = np.abs(rel)
    nf = np.maximum(n, 1).astype(np.float64)
    large = max_exact + (np.log(nf / max_exact) / math.log(MAX_DISTANCE / max_exact)
                         * (nb - max_exact)).astype(np.int64)
    large = np.minimum(large, nb - 1)
    return base + np.where(n < max_exact, n, large)


def _band_buckets(sub_len, dil):
    w = _band_window(sub_len)
    offsets = {}
    for a, ks, case in _band_blocks(sub_len):
        assert offsets.setdefault(case, ks - a) == ks - a
    i = np.arange(B_TQ)[:, None]
    j = np.arange(w)[None, :]
    tiles = []
    for case in sorted(offsets):
        rel = offsets[case] + j - i
        tiles.append(np.where(np.abs(rel) <= B_HALF, _t5_bucket(rel * dil), -1))
    return jnp.asarray(np.stack(tiles), jnp.int32)


def _band_bias_kernel(tab_ref, bucket_ref, o_ref, *, group):
    bucket = bucket_ref[...]
    for h in range(HEADS_PER_DIL):
        acc = jnp.full(bucket.shape, NEG_INF, F32)
        for kb in range(NUM_BUCKETS):
            acc = jnp.where(bucket == kb, tab_ref[kb, group * HEADS_PER_DIL + h] * LOG2E, acc)
        o_ref[:, h] = acc


def _band_bias(rel_bias, group, sub_len, dil):
    bucket = _band_buckets(sub_len, dil)
    ncase, tq, w = bucket.shape
    return pl.pallas_call(
        functools.partial(_band_bias_kernel, group=group),
        in_specs=[pl.BlockSpec(memory_space=pltpu.SMEM),
                  pl.BlockSpec(memory_space=pltpu.VMEM)],
        out_specs=pl.BlockSpec(memory_space=pltpu.VMEM),
        out_shape=jax.ShapeDtypeStruct((ncase, HEADS_PER_DIL, tq, w), F32),
        name=f"band_bias_dil{dil}",
    )(rel_bias.astype(F32), bucket)


def _attn_b_kernel(*refs, seq):
    qkv_refs, bias_refs = refs[:3 * N_DIL], refs[3 * N_DIL:4 * N_DIL]
    o_ref, o_scr, lse_scr = refs[4 * N_DIL:]
    tiles = [(g, dil, r, a, ks, case)
             for g, (_, dil) in enumerate(DIL_PAIRS)
             for r in range(dil)
             for a, ks, case in _band_blocks(seq // dil)]

    def rows(dil, r, start, n):
        return pl.ds(start, n) if dil == 1 else pl.ds(r + dil * start, n, stride=dil)

    def load_scores(tile):
        g, dil, r, a, ks, case = tile
        q_ref, k_ref, v_ref = qkv_refs[3 * g:3 * g + 3]
        w = _band_window(seq // dil)
        q = q_ref[rows(dil, r, a, B_TQ), :].astype(BF16)
        k = k_ref[rows(dil, r, ks, w), :].astype(BF16)
        v = v_ref[rows(dil, r, ks, w), :].astype(BF16)
        v1 = jnp.concatenate([v, jnp.ones_like(v)], axis=-1)
        return _qk(q, k) * (ATTN_SCALE * LOG2E) + bias_refs[g][case], v1

    def softmax_pv(s, v1):
        m = jnp.max(s, axis=-1, keepdims=True)
        p = jnp.exp2(s - m).astype(BF16)
        return jnp.dot(p, v1, preferred_element_type=F32), m

    def store(tile, ol, m):
        g, dil, r, a, _, _ = tile
        l = ol[:, HEAD_DIM:]
        o_scr[g, rows(dil, r, a, B_TQ), :] = ol[:, :HEAD_DIM] / l
        lse_scr[g, rows(dil, r, a, B_TQ), :] = m + jnp.log(l) * LOG2E

    n = len(tiles)
    scored, popped = {}, {}
    for step in range(n + 2 * B_SKEW):
        if step < n:
            scored[step] = load_scores(tiles[step])
        t = step - B_SKEW
        if 0 <= t < n:
            popped[t] = softmax_pv(*scored.pop(t))
        t = step - 2 * B_SKEW
        if 0 <= t < n:
            store(tiles[t], *popped.pop(t))
    lse = [lse_scr[g] for g in range(N_DIL)]
    mx = functools.reduce(jnp.maximum, lse)
    e = [jnp.exp2(x - mx) for x in lse]
    den = functools.reduce(lambda a, b: a + b, e)
    for g in range(N_DIL):
        o_ref[:, g * HEAD_DIM:(g + 1) * HEAD_DIM] = ((e[g] / den) * o_scr[g]).astype(o_ref.dtype)


def _attn_b(pa, pb, biases):
    b, s, _ = pa.shape

    def col(cols0, part, g):
        blk = (cols0 + (3 * g + part) * GROUP_W) // HEAD_DIM
        return pl.BlockSpec((None, s, HEAD_DIM), lambda bi, h: (bi, 0, blk + h))

    in_specs = [col(PA_B0, part, 0) for part in range(3)]
    operands = [pa] * 3
    for g in range(1, N_DIL):
        in_specs += [col(0, part, g - 1) for part in range(3)]
        operands += [pb] * 3
    for bias in biases:
        ncase, _, tq, w = bias.shape
        in_specs.append(pl.BlockSpec((ncase, None, tq, w), lambda bi, h: (0, h, 0, 0)))
    return pl.pallas_call(
        functools.partial(_attn_b_kernel, seq=s),
        grid=(b, HEADS_PER_DIL),
        in_specs=in_specs,
        out_specs=pl.BlockSpec((None, s, N_DIL * HEAD_DIM), lambda bi, h: (bi, 0, h)),
        out_shape=jax.ShapeDtypeStruct((b, s, WIDTH_B), BF16),
        scratch_shapes=[pltpu.VMEM((N_DIL, s, HEAD_DIM), F32), pltpu.VMEM((N_DIL, s, HEAD_DIM), F32)],
        compiler_params=_params("parallel", "parallel"),
        name="attn_b",
    )(*operands, *biases)


def _out_proj_kernel(oa_ref, ob_ref, om_ref, x_ref, ga_ref, gb_ref, gm_ref, w_ref, gpost_ref, *rest):
    if len(rest) == 3:
        cast_src_ref, o_ref, cast_dst_ref = rest
        cast_dst_ref[...] = cast_src_ref[...].astype(BF16)
    else:
        o_ref, = rest
    ob = jnp.concatenate([ob_ref[:, (h * N_DIL + g) * HEAD_DIM:(h * N_DIL + g + 1) * HEAD_DIM]
                          for g in range(N_DIL) for h in range(HEADS_PER_DIL)], axis=-1)
    mix = jnp.concatenate([_rms(oa_ref[...].astype(F32), ga_ref[...]).astype(BF16),
                           _rms(ob.astype(F32), gb_ref[...]).astype(BF16),
                           _rms(om_ref[...].astype(F32), gm_ref[...]).astype(BF16)], axis=-1)
    y = jnp.dot(mix, w_ref[...], preferred_element_type=F32)
    o_ref[...] = x_ref[...] + _rms(y, gpost_ref[...])


def _out_proj(oa, ob, om, x2d, ga, gb, gm, w, gpost, *, tm=512, cast=None):
    t, d = x2d.shape
    assert t % tm == 0

    def rows(width):
        return pl.BlockSpec((tm, width), lambda i: (i, 0))

    in_specs = [rows(WIDTH_A), rows(WIDTH_B), rows(WIDTH_M), rows(d),
                _resident(ga.shape), _resident(gb.shape), _resident(gm.shape),
                _resident(w.shape), _resident(gpost.shape)]
    out_specs = [rows(d)]
    out_shape = [jax.ShapeDtypeStruct((t, d), F32)]
    operands = [oa, ob, om, x2d, ga, gb, gm, w, gpost]
    if cast is not None:
        src_spec, dst_spec, dst_shape = _cast_rider(cast, (t // tm,))
        in_specs.append(src_spec)
        out_specs.append(dst_spec)
        out_shape.append(dst_shape)
        operands.append(cast)
    out = pl.pallas_call(
        _out_proj_kernel,
        grid=(t // tm,),
        in_specs=in_specs,
        out_specs=out_specs,
        out_shape=out_shape,
        compiler_params=_params("parallel"),
        name="out_proj",
    )(*operands)
    return out[0] if cast is None else tuple(out)


def _ffn_kernel(x_ref, gpre_ref, wup_ref, wdn_ref, gpost_ref, o_ref, h_scr):
    j = pl.program_id(1)
    last = pl.num_programs(1) - 1
    tm = x_ref.shape[0]

    def step(first, final):
        for rs in (_row_halves(tm) if first or final else (slice(0, tm),)):
            if first:
                h = _rms(x_ref[rs, :], gpre_ref[...]).astype(BF16)
                h_scr[rs, :] = h
            else:
                h = h_scr[rs, :]
            u = jnp.dot(h, wup_ref[...], preferred_element_type=F32)
            u = jnp.square(jnp.maximum(u, 0.0)).astype(BF16)
            acc = jnp.dot(u, wdn_ref[...], preferred_element_type=F32)
            if not first:
                acc = o_ref[rs, :] + acc
            if final:
                acc = x_ref[rs, :] + _rms(acc, gpost_ref[...])
            o_ref[rs, :] = acc

    pl.when(j == 0)(functools.partial(step, True, False))
    pl.when((j > 0) & (j < last))(functools.partial(step, False, False))
    pl.when(j == last)(functools.partial(step, False, True))


def _ffn(x2d, gpre, w_up, w_down, gpost, *, tm=1024, tf=FFN_CHUNK):
    t, d = x2d.shape
    d_ff = w_up.shape[1]
    assert t % tm == 0 and d_ff % tf == 0 and d_ff // tf >= 2
    return pl.pallas_call(
        _ffn_kernel,
        grid=(t // tm, d_ff // tf),
        in_specs=[
            pl.BlockSpec((tm, d), lambda i, j: (i, 0)),
            _resident(gpre.shape),
            pl.BlockSpec((d, tf), lambda i, j: (0, j)),
            pl.BlockSpec((tf, d), lambda i, j: (j, 0)),
            _resident(gpost.shape),
        ],
        out_specs=pl.BlockSpec((tm, d), lambda i, j: (i, 0)),
        out_shape=jax.ShapeDtypeStruct((t, d), F32),
        scratch_shapes=[pltpu.VMEM((tm, d), BF16)],
        compiler_params=_params("parallel", "arbitrary"),
        name="ffn",
    )(x2d, gpre, w_up, w_down, gpost)


def _rope_tables(seq_len):
    rows = seq_len // GRID_W
    row = np.repeat(np.arange(rows), GRID_W).astype(np.float64)
    col = np.tile(np.arange(GRID_W), rows).astype(np.float64)
    inv = ROPE_THETA ** (-(2.0 * np.arange(ROPE_PAIRS)) / ROPE_AXIS_DIM)
    ang_r = row[:, None] * inv
    ang_c = col[:, None] * inv
    cos = np.concatenate([np.cos(ang_r)] * 2 + [np.cos(ang_c)] * 2, axis=-1)
    sin = np.concatenate([-np.sin(ang_r), np.sin(ang_r), -np.sin(ang_c), np.sin(ang_c)], axis=-1)
    return jnp.asarray(cos, F32), jnp.asarray(sin, F32)


def _row(v):
    return v.reshape(1, -1).astype(F32)


def _encoder_layer(x, mem, p, biases, late_w):
    b, s, d = x.shape
    x2d = x.reshape(b * s, d)
    cast = late_w["w_up"].dtype != BF16
    cos_tab, sin_tab = _rope_tables(s)
    proj = _in_proj(x2d, p["pre_mix_norm"], p["w_in"], p["head_gain"], cos_tab, sin_tab, seq=s,
                    cast=late_w["w_up"] if cast else None)
    pa = proj[0].reshape(b, s, PA_COLS)
    pb = proj[1].reshape(b, s, PB_COLS)
    kv = _mem_kv(mem.reshape(-1, d), p["mem_norm"], p["w_mem_kv"]).reshape(b, mem.shape[1], -1)

    attn = _attn_am(pa, kv, cast=late_w["w_out"] if cast else None)
    oa, om = attn[0], attn[1]
    ob = _attn_b(pa, pb, biases)
    w_out = attn[2] if cast else late_w["w_out"]

    x1 = _out_proj(oa.reshape(b * s, WIDTH_A), ob.reshape(b * s, WIDTH_B), om.reshape(b * s, WIDTH_M), x2d,
                   p["out_norm_a"], p["out_norm_b"], p["out_norm_m"], w_out, p["post_mix_norm"],
                   cast=late_w["w_down"] if cast else None)
    if cast:
        x1, w_down = x1
        late_w = {"w_out": w_out, "w_up": proj[2], "w_down": w_down}
    y = _ffn(x1, p["pre_ffn_norm"], late_w["w_up"], late_w["w_down"], p["post_ffn_norm"])
    return y.reshape(b, s, d), late_w


def kernel(x_prompt, x_sample, mem_prompt, mem_sample, rel_bias, pre_mix_norm, w_in, q_norm_a, k_norm_a, mem_norm, w_mem_kv, out_norm_a, out_norm_b, out_norm_m, w_out, post_mix_norm, pre_ffn_norm, w_up, w_down, post_ffn_norm):
    depth = w_in.shape[0]
    assert x_prompt.shape[1] == x_sample.shape[1]
    seq = x_prompt.shape[1]
    biases = [_band_bias(rel_bias, g, seq // dil, dil) for g, (_, dil) in enumerate(DIL_PAIRS)]
    layers = []
    for l in range(depth):
        layers.append({
            "pre_mix_norm": _row(pre_mix_norm[l]),
            "w_in": w_in[l].astype(BF16),
            "head_gain": _row(jnp.concatenate([jnp.tile(q_norm_a[l], N_HEADS_A),
                                               jnp.tile(k_norm_a[l], N_KV_A)])),
            "mem_norm": _row(mem_norm[l]),
            "w_mem_kv": w_mem_kv[l].astype(BF16),
            "out_norm_a": _row(out_norm_a[l]),
            "out_norm_b": _row(out_norm_b[l]),
            "out_norm_m": _row(out_norm_m[l]),
            "post_mix_norm": _row(post_mix_norm[l]),
            "pre_ffn_norm": _row(pre_ffn_norm[l]),
            "post_ffn_norm": _row(post_ffn_norm[l]),
        })

    xp, xs = x_prompt, x_sample
    for l, p in enumerate(layers):
        late_w = {"w_out": w_out[l], "w_up": w_up[l], "w_down": w_down[l]}
        xp, late_w = _encoder_layer(xp, mem_prompt, p, biases, late_w)
        xs, _ = _encoder_layer(xs, mem_sample, p, biases, late_w)
    return (xp, xs)
```

```python
import functools
import math

import jax
import jax.numpy as jnp
import numpy as np
from jax import lax
from jax.experimental import pallas as pl
from jax.experimental.pallas import tpu as pltpu

F32 = jnp.float32
BF16 = jnp.bfloat16

HEAD_DIM = 128
N_HEADS_A = 6
N_KV_A = 2
GQA_GROUP = N_HEADS_A // N_KV_A
DIL_PAIRS = ((128, 1), (512, 4), (2048, 16))
N_DIL = len(DIL_PAIRS)
HEADS_PER_DIL = 2
N_HEADS_B = N_DIL * HEADS_PER_DIL
N_HEADS_M = 4
WIDTH_A = N_HEADS_A * HEAD_DIM
WIDTH_KV_A = N_KV_A * HEAD_DIM
WIDTH_B = N_HEADS_B * HEAD_DIM
WIDTH_M = N_HEADS_M * HEAD_DIM
GROUP_W = HEADS_PER_DIL * HEAD_DIM
GRID_W = 64
ROPE_THETA = 10000.0
ROPE_AXIS_DIM = HEAD_DIM // 2
ROPE_PAIRS = ROPE_AXIS_DIM // 2
NUM_BUCKETS = 32
MAX_DISTANCE = 1024
EPS = 1e-6
NEG_INF = -1e30
ATTN_SCALE = HEAD_DIM ** -0.5
LOG2E = math.log2(math.e)

PA_QA = 0
PA_KA = PA_QA + WIDTH_A
ROPE_COLS = PA_KA + WIDTH_KV_A
PA_QM = ROPE_COLS
PA_VA = PA_QM + WIDTH_M
PA_B0 = PA_VA + WIDTH_KV_A
PA_COLS = PA_B0 + 3 * GROUP_W
PB_COLS = (N_DIL - 1) * 3 * GROUP_W
assert PA_QA % (GQA_GROUP * HEAD_DIM) == 0 and PA_QM % WIDTH_M == 0
IN_COLS = PA_COLS + PB_COLS
IN_CHUNK = 512
FFN_CHUNK = 1024

BF16_SUBLANES = 16
V7X_VMEM_BYTES = 64 * 1024 * 1024
VMEM_LIMIT_BYTES = V7X_VMEM_BYTES - 4 * 1024 * 1024


def _params(*semantics):
    return pltpu.CompilerParams(dimension_semantics=semantics,
                                vmem_limit_bytes=VMEM_LIMIT_BYTES)


def _resident(shape):
    zeros = (0,) * len(shape)
    return pl.BlockSpec(shape, lambda *_: zeros, pipeline_mode=pl.Buffered(1))


def _rms(v, g):
    ms = jnp.mean(v * v, axis=-1, keepdims=True)
    return v * lax.rsqrt(ms + EPS) * g


def _row_halves(rows):
    return (slice(0, rows // 2), slice(rows // 2, rows))


def _qk(q, k):
    return lax.dot_general(q, k, (((1,), (1,)), ((), ())), preferred_element_type=F32)


def _cast_rider(weight, grid):
    rows, cols = weight.shape
    n_steps = math.prod(grid)
    assert rows % (n_steps * BF16_SUBLANES) == 0

    def slab(*idx):
        flat = 0
        for i, n in zip(idx, grid):
            flat = flat * n + i
        return (flat, 0)

    spec = pl.BlockSpec((rows // n_steps, cols), slab)
    return spec, spec, jax.ShapeDtypeStruct((rows, cols), BF16)


def _in_column_runs():
    starts = np.cumsum([0, WIDTH_A, WIDTH_KV_A, WIDTH_KV_A, WIDTH_B, WIDTH_B, WIDTH_B])
    qa, ka, va, qb, kb, vb, qm = (int(c) for c in starts)
    segs = [(qa, WIDTH_A), (ka, WIDTH_KV_A), (qm, WIDTH_M), (va, WIDTH_KV_A)]
    for g in range(N_DIL):
        segs += [(part + g * GROUP_W, GROUP_W) for part in (qb, kb, vb)]
    cols = np.concatenate([np.arange(s, s + w) for s, w in segs])
    assert cols.size == IN_COLS and np.array_equal(np.sort(cols), np.arange(IN_COLS))
    chunks = []
    for c0 in range(0, IN_COLS, IN_CHUNK):
        piece = cols[c0:c0 + IN_CHUNK]
        breaks = [0] + [int(i) + 1 for i in np.nonzero(np.diff(piece) != 1)[0]] + [IN_CHUNK]
        chunks.append([(int(piece[a]), b - a) for a, b in zip(breaks[:-1], breaks[1:])])
    return chunks


def _in_proj_kernel(x_ref, g_ref, w_ref, hg_ref, cos_ref, sin_ref, *rest):
    n_cast = (len(rest) - 2) // 2
    pa_ref, pb_ref = rest[n_cast:n_cast + 2]
    for src_ref, dst_ref in zip(rest[:n_cast], rest[n_cast + 2:]):
        dst_ref[...] = src_ref[...].astype(BF16)
    h = _rms(x_ref[...], g_ref[...]).astype(BF16)
    tm = h.shape[0]
    cos = cos_ref[...]
    sin = sin_ref[...]
    lane = lax.broadcasted_iota(jnp.int32, (tm, HEAD_DIM), 1)
    low_half = (lane % ROPE_AXIS_DIM) < ROPE_PAIRS
    for chunk, runs in enumerate(_in_column_runs()):
        c0 = chunk * IN_CHUNK
        parts = [jnp.dot(h, w_ref[:, s:s + n], preferred_element_type=F32) for s, n in runs]
        acc = parts[0] if len(parts) == 1 else jnp.concatenate(parts, axis=-1)
        if c0 < ROPE_COLS:
            for hh in range(IN_CHUNK // HEAD_DIM):
                cs = slice(c0 + hh * HEAD_DIM, c0 + (hh + 1) * HEAD_DIM)
                y = _rms(acc[:, hh * HEAD_DIM:(hh + 1) * HEAD_DIM], hg_ref[:, cs])
                partner = jnp.where(low_half, pltpu.roll(y, HEAD_DIM - ROPE_PAIRS, 1),
                                    pltpu.roll(y, ROPE_PAIRS, 1))
                pa_ref[:, cs] = (y * cos + partner * sin).astype(BF16)
        elif c0 < PA_COLS:
            pa_ref[:, c0:c0 + IN_CHUNK] = acc.astype(BF16)
        else:
            pb_ref[:, c0 - PA_COLS:c0 - PA_COLS + IN_CHUNK] = acc


def _in_proj(x2d, gain, w, head_gain, cos_tab, sin_tab, *, seq, tm=512, casts=()):
    t, d = x2d.shape
    assert t % tm == 0 and seq % tm == 0
    assert ROPE_COLS % IN_CHUNK == 0 and PA_COLS % IN_CHUNK == 0 and IN_COLS % IN_CHUNK == 0
    pos_blocks = seq // tm
    in_specs = [
        pl.BlockSpec((tm, d), lambda i: (i, 0)),
        _resident(gain.shape),
        _resident(w.shape),
        _resident(head_gain.shape),
        pl.BlockSpec((tm, HEAD_DIM), lambda i: (i % pos_blocks, 0)),
        pl.BlockSpec((tm, HEAD_DIM), lambda i: (i % pos_blocks, 0)),
    ]
    out_specs = [pl.BlockSpec((tm, PA_COLS), lambda i: (i, 0)),
                 pl.BlockSpec((tm, PB_COLS), lambda i: (i, 0))]
    out_shape = [jax.ShapeDtypeStruct((t, PA_COLS), BF16),
                 jax.ShapeDtypeStruct((t, PB_COLS), F32)]
    operands = [x2d, gain, w, head_gain, cos_tab, sin_tab]
    for cast in casts:
        src_spec, dst_spec, dst_shape = _cast_rider(cast, (t // tm,))
        in_specs.append(src_spec)
        out_specs.append(dst_spec)
        out_shape.append(dst_shape)
        operands.append(cast)
    return pl.pallas_call(
        _in_proj_kernel,
        grid=(t // tm,),
        in_specs=in_specs,
        out_specs=out_specs,
        out_shape=out_shape,
        compiler_params=_params("parallel"),
        name="in_proj",
    )(*operands)


def _norm_matmul_kernel(x_ref, g_ref, w_ref, o_ref, h_scr):
    @pl.when(pl.program_id(1) == 0)
    def _():
        h_scr[...] = _rms(x_ref[...], g_ref[...]).astype(BF16)

    o_ref[...] = jnp.dot(h_scr[...], w_ref[...], preferred_element_type=F32).astype(o_ref.dtype)


def _mem_kv(mem2d, gain, w, *, tm=1024, tn=1024):
    t, d = mem2d.shape
    n = w.shape[1]
    assert t % tm == 0 and n % tn == 0
    return pl.pallas_call(
        _norm_matmul_kernel,
        grid=(t // tm, n // tn),
        in_specs=[
            pl.BlockSpec((tm, d), lambda i, j: (i, 0)),
            pl.BlockSpec((1, d), lambda i, j: (0, 0)),
            pl.BlockSpec((d, tn), lambda i, j: (0, j)),
        ],
        out_specs=pl.BlockSpec((tm, tn), lambda i, j: (i, j)),
        out_shape=jax.ShapeDtypeStruct((t, n), BF16),
        scratch_shapes=[pltpu.VMEM((tm, d), BF16)],
        compiler_params=_params("parallel", "arbitrary"),
        name="mem_kv",
    )(mem2d, gain, w)


def _attn_am_kernel(qa_ref, ka_ref, va_ref, qm_ref, kvm_ref, *rest):
    if len(rest) == 6:
        cast_src_ref, oa_ref, om_ref, cast_dst_ref, va1_scr, vm1_scr = rest
        cast_dst_ref[...] = cast_src_ref[...].astype(BF16)
    else:
        oa_ref, om_ref, va1_scr, vm1_scr = rest

    @pl.when(pl.program_id(1) == 0)
    def _():
        for kv in range(N_KV_A):
            va1_scr[kv, :, :HEAD_DIM] = va_ref[:, kv * HEAD_DIM:(kv + 1) * HEAD_DIM]
            va1_scr[kv, :, HEAD_DIM:] = jnp.ones((va_ref.shape[0], HEAD_DIM), BF16)
        for h in range(N_HEADS_M):
            vm1_scr[h, :, :HEAD_DIM] = kvm_ref[:, WIDTH_M + h * HEAD_DIM:WIDTH_M + (h + 1) * HEAD_DIM]
            vm1_scr[h, :, HEAD_DIM:] = jnp.ones((kvm_ref.shape[0], HEAD_DIM), BF16)

    def head_cols(h):
        return slice(h * HEAD_DIM, (h + 1) * HEAD_DIM)

    units = [(qa_ref, ka_ref, head_cols(h // GQA_GROUP), (va1_scr, h // GQA_GROUP), oa_ref, head_cols(h))
             for h in range(N_HEADS_A)]
    units += [(qm_ref, kvm_ref, head_cols(h), (vm1_scr, h), om_ref, head_cols(h)) for h in range(N_HEADS_M)]

    def scores(unit):
        q_ref, k_ref, kcols, _, _, cs = unit
        return _qk(q_ref[:, cs], k_ref[:, kcols])

    s_next = scores(units[0])
    for idx, unit in enumerate(units):
        s = s_next
        if idx + 1 < len(units):
            s_next = scores(units[idx + 1])
        _, _, _, (v1_scr, vi), o_ref, cs = unit
        m = jnp.max(s, axis=-1, keepdims=True)
        p = jnp.exp2((s - m) * (ATTN_SCALE * LOG2E)).astype(BF16)
        ol = jnp.dot(p, v1_scr[vi], preferred_element_type=F32)
        o_ref[:, cs] = (ol[:, :HEAD_DIM] / ol[:, HEAD_DIM:]).astype(o_ref.dtype)


def _attn_am(pa, kvm, *, tq=1024, cast=None):
    b, s, _ = pa.shape
    n_mem = kvm.shape[1]
    assert s % tq == 0 and PA_KA % WIDTH_KV_A == 0 and PA_VA % WIDTH_KV_A == 0
    grid = (b, s // tq)
    in_specs = [
        pl.BlockSpec((None, tq, WIDTH_A), lambda bi, i: (bi, i, PA_QA // WIDTH_A)),
        pl.BlockSpec((None, s, WIDTH_KV_A), lambda bi, i: (bi, 0, PA_KA // WIDTH_KV_A)),
        pl.BlockSpec((None, s, WIDTH_KV_A), lambda bi, i: (bi, 0, PA_VA // WIDTH_KV_A)),
        pl.BlockSpec((None, tq, WIDTH_M), lambda bi, i: (bi, i, PA_QM // WIDTH_M)),
        pl.BlockSpec((None, n_mem, 2 * WIDTH_M), lambda bi, i: (bi, 0, 0)),
    ]
    out_specs = [pl.BlockSpec((None, tq, WIDTH_A), lambda bi, i: (bi, i, 0)),
                 pl.BlockSpec((None, tq, WIDTH_M), lambda bi, i: (bi, i, 0))]
    out_shape = [jax.ShapeDtypeStruct((b, s, WIDTH_A), BF16),
                 jax.ShapeDtypeStruct((b, s, WIDTH_M), BF16)]
    operands = [pa, pa, pa, pa, kvm]
    if cast is not None:
        src_spec, dst_spec, dst_shape = _cast_rider(cast, grid)
        in_specs.append(src_spec)
        out_specs.append(dst_spec)
        out_shape.append(dst_shape)
        operands.append(cast)
    return pl.pallas_call(
        _attn_am_kernel,
        grid=grid,
        in_specs=in_specs,
        out_specs=out_specs,
        out_shape=out_shape,
        scratch_shapes=[pltpu.VMEM((N_KV_A, s, 2 * HEAD_DIM), BF16),
                        pltpu.VMEM((N_HEADS_M, n_mem, 2 * HEAD_DIM), BF16)],
        compiler_params=_params("parallel", "arbitrary"),
        name="attn_am",
    )(*operands)


B_TQ = 128
B_HALF = 64
B_SKEW = 2
assert all(w // (2 * d) == B_HALF for w, d in DIL_PAIRS)


def _band_window(sub_len):
    return min(sub_len, B_TQ + 2 * B_HALF)


def _band_blocks(sub_len):
    w = _band_window(sub_len)
    nblk = sub_len // B_TQ
    out = []
    for qb in range(nblk):
        a = qb * B_TQ
        ks = min(max(a - B_HALF, 0), sub_len - w)
        case = 0 if qb == 0 else (2 if qb == nblk - 1 else 1)
        out.append((a, ks, case))
    return out


def _t5_bucket(rel):
    nb = NUM_BUCKETS // 2
    max_exact = nb // 2
    base = np.where(rel > 0, nb, 0)
    n = np.abs(rel)
    nf = np.maximum(n, 1).astype(np.float64)
    large = max_exact + (np.log(nf / max_exact) / math.log(MAX_DISTANCE / max_exact)
                         * (nb - max_exact)).astype(np.int64)
    large = np.minimum(large, nb - 1)
    return base + np.where(n < max_exact, n, large)


def _band_buckets(sub_len, dil):
    w = _band_window(sub_len)
    offsets = {}
    for a, ks, case in _band_blocks(sub_len):
        assert offsets.setdefault(case, ks - a) == ks - a
    i = np.arange(B_TQ)[:, None]
    j = np.arange(w)[None, :]
    tiles = []
    for case in sorted(offsets):
        rel = offsets[case] + j - i
        tiles.append(np.where(np.abs(rel) <= B_HALF, _t5_bucket(rel * dil), -1))
    return jnp.asarray(np.stack(tiles), jnp.int32)


def _band_bias_kernel(tab_ref, *refs):
    for group, (bucket_ref, o_ref) in enumerate(zip(refs[:N_DIL], refs[N_DIL:])):
        bucket = bucket_ref[...]
        for h in range(HEADS_PER_DIL):
            acc = jnp.full(bucket.shape, NEG_INF, F32)
            for kb in range(NUM_BUCKETS):
                acc = jnp.where(bucket == kb, tab_ref[kb, group * HEADS_PER_DIL + h] * LOG2E, acc)
            o_ref[:, h] = acc


def _band_biases(rel_bias, seq):
    buckets = [_band_buckets(seq // dil, dil) for _, dil in DIL_PAIRS]
    vmem = pl.BlockSpec(memory_space=pltpu.VMEM)
    return pl.pallas_call(
        _band_bias_kernel,
        in_specs=[pl.BlockSpec(memory_space=pltpu.SMEM)] + [vmem] * N_DIL,
        out_specs=[vmem] * N_DIL,
        out_shape=[jax.ShapeDtypeStruct((bk.shape[0], HEADS_PER_DIL) + bk.shape[1:], F32) for bk in buckets],
        name="band_bias",
    )(rel_bias.astype(F32), *buckets)


def _attn_b_kernel(*refs, seq):
    qkv_refs, bias_refs = refs[:3 * N_DIL], refs[3 * N_DIL:4 * N_DIL]
    o_ref, o_scr, lse_scr = refs[4 * N_DIL:]
    tiles = [(g, dil, r, a, ks, case)
             for g, (_, dil) in enumerate(DIL_PAIRS)
             for r in range(dil)
             for a, ks, case in _band_blocks(seq // dil)]

    def rows(dil, r, start, n):
        return pl.ds(start, n) if dil == 1 else pl.ds(r + dil * start, n, stride=dil)

    def load_scores(tile):
        g, dil, r, a, ks, case = tile
        q_ref, k_ref, v_ref = qkv_refs[3 * g:3 * g + 3]
        w = _band_window(seq // dil)
        q = q_ref[rows(dil, r, a, B_TQ), :].astype(BF16)
        k = k_ref[rows(dil, r, ks, w), :].astype(BF16)
        v = v_ref[rows(dil, r, ks, w), :].astype(BF16)
        v1 = jnp.concatenate([v, jnp.ones_like(v)], axis=-1)
        return _qk(q, k) * (ATTN_SCALE * LOG2E) + bias_refs[g][case], v1

    def softmax_pv(s, v1):
        m = jnp.max(s, axis=-1, keepdims=True)
        p = jnp.exp2(s - m).astype(BF16)
        return jnp.dot(p, v1, preferred_element_type=F32), m

    def store(tile, ol, m):
        g, dil, r, a, _, _ = tile
        l = ol[:, HEAD_DIM:]
        o_scr[g, rows(dil, r, a, B_TQ), :] = ol[:, :HEAD_DIM] / l
        lse_scr[g, rows(dil, r, a, B_TQ), :] = m + jnp.log(l) * LOG2E

    n = len(tiles)
    scored, popped = {}, {}
    for step in range(n + 2 * B_SKEW):
        if step < n:
            scored[step] = load_scores(tiles[step])
        t = step - B_SKEW
        if 0 <= t < n:
            popped[t] = softmax_pv(*scored.pop(t))
        t = step - 2 * B_SKEW
        if 0 <= t < n:
            store(tiles[t], *popped.pop(t))
    lse = [lse_scr[g] for g in range(N_DIL)]
    mx = functools.reduce(jnp.maximum, lse)
    e = [jnp.exp2(x - mx) for x in lse]
    den = functools.reduce(lambda a, b: a + b, e)
    for g in range(N_DIL):
        o_ref[:, g * HEAD_DIM:(g + 1) * HEAD_DIM] = ((e[g] / den) * o_scr[g]).astype(o_ref.dtype)


def _attn_b(pa, pb, biases):
    b, s, _ = pa.shape

    def col(cols0, part, g):
        blk = (cols0 + (3 * g + part) * GROUP_W) // HEAD_DIM
        return pl.BlockSpec((None, s, HEAD_DIM), lambda bi, h: (bi, 0, blk + h))

    in_specs = [col(PA_B0, part, 0) for part in range(3)]
    operands = [pa] * 3
    for g in range(1, N_DIL):
        in_specs += [col(0, part, g - 1) for part in range(3)]
        operands += [pb] * 3
    for bias in biases:
        ncase, _, tq, w = bias.shape
        in_specs.append(pl.BlockSpec((ncase, None, tq, w), lambda bi, h: (0, h, 0, 0)))
    return pl.pallas_call(
        functools.partial(_attn_b_kernel, seq=s),
        grid=(b, HEADS_PER_DIL),
        in_specs=in_specs,
        out_specs=pl.BlockSpec((None, s, N_DIL * HEAD_DIM), lambda bi, h: (bi, 0, h)),
        out_shape=jax.ShapeDtypeStruct((b, s, WIDTH_B), BF16),
        scratch_shapes=[pltpu.VMEM((N_DIL, s, HEAD_DIM), F32), pltpu.VMEM((N_DIL, s, HEAD_DIM), F32)],
        compiler_params=_params("parallel", "parallel"),
        name="attn_b",
    )(*operands, *biases)


def _out_proj_kernel(oa_ref, ob_ref, om_ref, x_ref, ga_ref, gb_ref, gm_ref, w_ref, gpost_ref, *rest):
    if len(rest) == 3:
        cast_src_ref, o_ref, cast_dst_ref = rest
        cast_dst_ref[...] = cast_src_ref[...].astype(BF16)
    else:
        o_ref, = rest
    ob = jnp.concatenate([ob_ref[:, (h * N_DIL + g) * HEAD_DIM:(h * N_DIL + g + 1) * HEAD_DIM]
                          for g in range(N_DIL) for h in range(HEADS_PER_DIL)], axis=-1)
    mix = jnp.concatenate([_rms(oa_ref[...].astype(F32), ga_ref[...]).astype(BF16),
                           _rms(ob.astype(F32), gb_ref[...]).astype(BF16),
                           _rms(om_ref[...].astype(F32), gm_ref[...]).astype(BF16)], axis=-1)
    y = jnp.dot(mix, w_ref[...], preferred_element_type=F32)
    o_ref[...] = x_ref[...] + _rms(y, gpost_ref[...])


def _out_proj(oa, ob, om, x2d, ga, gb, gm, w, gpost, *, tm=512, cast=None):
    t, d = x2d.shape
    assert t % tm == 0

    def rows(width):
        return pl.BlockSpec((tm, width), lambda i: (i, 0))

    in_specs = [rows(WIDTH_A), rows(WIDTH_B), rows(WIDTH_M), rows(d),
                _resident(ga.shape), _resident(gb.shape), _resident(gm.shape),
                _resident(w.shape), _resident(gpost.shape)]
    out_specs = [rows(d)]
    out_shape = [jax.ShapeDtypeStruct((t, d), F32)]
    operands = [oa, ob, om, x2d, ga, gb, gm, w, gpost]
    if cast is not None:
        src_spec, dst_spec, dst_shape = _cast_rider(cast, (t // tm,))
        in_specs.append(src_spec)
        out_specs.append(dst_spec)
        out_shape.append(dst_shape)
        operands.append(cast)
    out = pl.pallas_call(
        _out_proj_kernel,
        grid=(t // tm,),
        in_specs=in_specs,
        out_specs=out_specs,
        out_shape=out_shape,
        compiler_params=_params("parallel"),
        name="out_proj",
    )(*operands)
    return out[0] if cast is None else tuple(out)


def _ffn_kernel(x_ref, gpre_ref, wup_ref, wdn_ref, gpost_ref, o_ref, h_scr):
    j = pl.program_id(1)
    last = pl.num_programs(1) - 1
    tm = x_ref.shape[0]

    def step(first, final):
        for rs in (_row_halves(tm) if first or final else (slice(0, tm),)):
            if first:
                h = _rms(x_ref[rs, :], gpre_ref[...]).astype(BF16)
                h_scr[rs, :] = h
            else:
                h = h_scr[rs, :]
            u = jnp.dot(h, wup_ref[...], preferred_element_type=F32)
            u = jnp.square(jnp.maximum(u, 0.0)).astype(BF16)
            acc = jnp.dot(u, wdn_ref[...], preferred_element_type=F32)
            if not first:
                acc = o_ref[rs, :] + acc
            if final:
                acc = x_ref[rs, :] + _rms(acc, gpost_ref[...])
            o_ref[rs, :] = acc

    pl.when(j == 0)(functools.partial(step, True, False))
    pl.when((j > 0) & (j < last))(functools.partial(step, False, False))
    pl.when(j == last)(functools.partial(step, False, True))


def _ffn(x2d, gpre, w_up, w_down, gpost, *, tm=1024, tf=FFN_CHUNK):
    t, d = x2d.shape
    d_ff = w_up.shape[1]
    assert t % tm == 0 and d_ff % tf == 0 and d_ff // tf >= 2
    return pl.pallas_call(
        _ffn_kernel,
        grid=(t // tm, d_ff // tf),
        in_specs=[
            pl.BlockSpec((tm, d), lambda i, j: (i, 0)),
            _resident(gpre.shape),
            pl.BlockSpec((d, tf), lambda i, j: (0, j)),
            pl.BlockSpec((tf, d), lambda i, j: (j, 0)),
            _resident(gpost.shape),
        ],
        out_specs=pl.BlockSpec((tm, d), lambda i, j: (i, 0)),
        out_shape=jax.ShapeDtypeStruct((t, d), F32),
        scratch_shapes=[pltpu.VMEM((tm, d), BF16)],
        compiler_params=_params("parallel", "arbitrary"),
        name="ffn",
    )(x2d, gpre, w_up, w_down, gpost)


def _rope_tables(seq_len):
    rows = seq_len // GRID_W
    row = np.repeat(np.arange(rows), GRID_W).astype(np.float64)
    col = np.tile(np.arange(GRID_W), rows).astype(np.float64)
    inv = ROPE_THETA ** (-(2.0 * np.arange(ROPE_PAIRS)) / ROPE_AXIS_DIM)
    ang_r = row[:, None] * inv
    ang_c = col[:, None] * inv
    cos = np.concatenate([np.cos(ang_r)] * 2 + [np.cos(ang_c)] * 2, axis=-1)
    sin = np.concatenate([-np.sin(ang_r), np.sin(ang_r), -np.sin(ang_c), np.sin(ang_c)], axis=-1)
    return jnp.asarray(cos, F32), jnp.asarray(sin, F32)


def _row(v):
    return v.reshape(1, -1).astype(F32)


def _encoder_layer(x, mem, p, biases, late_w):
    b, s, d = x.shape
    x2d = x.reshape(b * s, d)
    cast = late_w["w_up"].dtype != BF16
    cos_tab, sin_tab = _rope_tables(s)
    proj = _in_proj(x2d, p["pre_mix_norm"], p["w_in"], p["head_gain"], cos_tab, sin_tab, seq=s,
                    casts=(late_w["w_up"], late_w["w_mem_kv"]) if cast else ())
    pa = proj[0].reshape(b, s, PA_COLS)
    pb = proj[1].reshape(b, s, PB_COLS)
    w_mem_kv = proj[3] if cast else late_w["w_mem_kv"]
    kv = _mem_kv(mem.reshape(-1, d), p["mem_norm"], w_mem_kv).reshape(b, mem.shape[1], -1)

    attn = _attn_am(pa, kv, cast=late_w["w_out"] if cast else None)
    oa, om = attn[0], attn[1]
    ob = _attn_b(pa, pb, biases)
    w_out = attn[2] if cast else late_w["w_out"]

    x1 = _out_proj(oa.reshape(b * s, WIDTH_A), ob.reshape(b * s, WIDTH_B), om.reshape(b * s, WIDTH_M), x2d,
                   p["out_norm_a"], p["out_norm_b"], p["out_norm_m"], w_out, p["post_mix_norm"],
                   cast=late_w["w_down"] if cast else None)
    if cast:
        x1, w_down = x1
        late_w = {"w_mem_kv": w_mem_kv, "w_out": w_out, "w_up": proj[2], "w_down": w_down}
    y = _ffn(x1, p["pre_ffn_norm"], late_w["w_up"], late_w["w_down"], p["post_ffn_norm"])
    return y.reshape(b, s, d), late_w


def kernel(x_prompt, x_sample, mem_prompt, mem_sample, rel_bias, pre_mix_norm, w_in, q_norm_a, k_norm_a, mem_norm, w_mem_kv, out_norm_a, out_norm_b, out_norm_m, w_out, post_mix_norm, pre_ffn_norm, w_up, w_down, post_ffn_norm):
    depth = w_in.shape[0]
    assert x_prompt.shape[1] == x_sample.shape[1]
    seq = x_prompt.shape[1]
    biases = _band_biases(rel_bias, seq)
    layers = []
    for l in range(depth):
        layers.append({
            "pre_mix_norm": _row(pre_mix_norm[l]),
            "w_in": w_in[l].astype(BF16),
            "head_gain": _row(jnp.concatenate([jnp.tile(q_norm_a[l], N_HEADS_A),
                                               jnp.tile(k_norm_a[l], N_KV_A)])),
            "mem_norm": _row(mem_norm[l]),
            "out_norm_a": _row(out_norm_a[l]),
            "out_norm_b": _row(out_norm_b[l]),
            "out_norm_m": _row(out_norm_m[l]),
            "post_mix_norm": _row(post_mix_norm[l]),
            "pre_ffn_norm": _row(pre_ffn_norm[l]),
            "post_ffn_norm": _row(post_ffn_norm[l]),
        })

    xp, xs = x_prompt, x_sample
    for l, p in enumerate(layers):
        late_w = {"w_mem_kv": w_mem_kv[l], "w_out": w_out[l], "w_up": w_up[l], "w_down": w_down[l]}
        xp, late_w = _encoder_layer(xp, mem_prompt, p, biases, late_w)
        xs, _ = _encoder_layer(xs, mem_sample, p, biases, late_w)
    return (xp, xs)
```

```python
import functools
import math

import jax
import jax.numpy as jnp
import numpy as np
from jax import lax
from jax.experimental import pallas as pl
from jax.experimental.pallas import tpu as pltpu

F32 = jnp.float32
BF16 = jnp.bfloat16

HEAD_DIM = 128
N_HEADS_A = 6
N_KV_A = 2
GQA_GROUP = N_HEADS_A // N_KV_A
DIL_PAIRS = ((128, 1), (512, 4), (2048, 16))
N_DIL = len(DIL_PAIRS)
HEADS_PER_DIL = 2
N_HEADS_B = N_DIL * HEADS_PER_DIL
N_HEADS_M = 4
WIDTH_A = N_HEADS_A * HEAD_DIM
WIDTH_KV_A = N_KV_A * HEAD_DIM
WIDTH_B = N_HEADS_B * HEAD_DIM
WIDTH_M = N_HEADS_M * HEAD_DIM
GROUP_W = HEADS_PER_DIL * HEAD_DIM
GRID_W = 64
ROPE_THETA = 10000.0
ROPE_AXIS_DIM = HEAD_DIM // 2
ROPE_PAIRS = ROPE_AXIS_DIM // 2
NUM_BUCKETS = 32
MAX_DISTANCE = 1024
EPS = 1e-6
NEG_INF = -1e30
ATTN_SCALE = HEAD_DIM ** -0.5
LOG2E = math.log2(math.e)

PA_QA = 0
PA_KA = PA_QA + WIDTH_A
ROPE_COLS = PA_KA + WIDTH_KV_A
PA_QM = ROPE_COLS
PA_VA = PA_QM + WIDTH_M
PA_B0 = PA_VA + WIDTH_KV_A
PA_COLS = PA_B0 + 3 * GROUP_W
PB_COLS = (N_DIL - 1) * 3 * GROUP_W
assert PA_QA % (GQA_GROUP * HEAD_DIM) == 0 and PA_QM % WIDTH_M == 0
IN_COLS = PA_COLS + PB_COLS
IN_CHUNK = 512
Q_SCALE = ATTN_SCALE * LOG2E


def _is_query_col(col):
    if PA_QM <= col < PA_QM + WIDTH_M or col == PA_B0:
        return True
    return col >= PA_COLS and (col - PA_COLS) % (3 * GROUP_W) == 0

FFN_CHUNK = 1024

BF16_SUBLANES = 16
V7X_VMEM_BYTES = 64 * 1024 * 1024
VMEM_LIMIT_BYTES = V7X_VMEM_BYTES - 4 * 1024 * 1024


def _params(*semantics):
    return pltpu.CompilerParams(dimension_semantics=semantics,
                                vmem_limit_bytes=VMEM_LIMIT_BYTES)


def _resident(shape):
    zeros = (0,) * len(shape)
    return pl.BlockSpec(shape, lambda *_: zeros, pipeline_mode=pl.Buffered(1))


def _rms(v, g):
    ms = jnp.mean(v * v, axis=-1, keepdims=True)
    return v * lax.rsqrt(ms + EPS) * g


def _row_halves(rows):
    return (slice(0, rows // 2), slice(rows // 2, rows))


def _qk(q, k):
    return lax.dot_general(q, k, (((1,), (1,)), ((), ())), preferred_element_type=F32)


def _cast_rider(weight, grid):
    rows, cols = weight.shape
    n_steps = math.prod(grid)
    assert rows % (n_steps * BF16_SUBLANES) == 0

    def slab(*idx):
        flat = 0
        for i, n in zip(idx, grid):
            flat = flat * n + i
        return (flat, 0)

    spec = pl.BlockSpec((rows // n_steps, cols), slab)
    return spec, spec, jax.ShapeDtypeStruct((rows, cols), BF16)


def _in_column_runs():
    starts = np.cumsum([0, WIDTH_A, WIDTH_KV_A, WIDTH_KV_A, WIDTH_B, WIDTH_B, WIDTH_B])
    qa, ka, va, qb, kb, vb, qm = (int(c) for c in starts)
    segs = [(qa, WIDTH_A), (ka, WIDTH_KV_A), (qm, WIDTH_M), (va, WIDTH_KV_A)]
    for g in range(N_DIL):
        segs += [(part + g * GROUP_W, GROUP_W) for part in (qb, kb, vb)]
    cols = np.concatenate([np.arange(s, s + w) for s, w in segs])
    assert cols.size == IN_COLS and np.array_equal(np.sort(cols), np.arange(IN_COLS))
    chunks = []
    for c0 in range(0, IN_COLS, IN_CHUNK):
        piece = cols[c0:c0 + IN_CHUNK]
        breaks = [0] + [int(i) + 1 for i in np.nonzero(np.diff(piece) != 1)[0]] + [IN_CHUNK]
        chunks.append([(int(piece[a]), b - a) for a, b in zip(breaks[:-1], breaks[1:])])
    return chunks


def _in_proj_kernel(x_ref, g_ref, w_ref, hg_ref, cos_ref, sin_ref, *rest):
    n_cast = (len(rest) - 2) // 2
    pa_ref, pb_ref = rest[n_cast:n_cast + 2]
    for src_ref, dst_ref in zip(rest[:n_cast], rest[n_cast + 2:]):
        dst_ref[...] = src_ref[...].astype(BF16)
    h = _rms(x_ref[...], g_ref[...]).astype(BF16)
    tm = h.shape[0]
    cos = cos_ref[...]
    sin = sin_ref[...]
    lane = lax.broadcasted_iota(jnp.int32, (tm, HEAD_DIM), 1)
    low_half = (lane % ROPE_AXIS_DIM) < ROPE_PAIRS
    for chunk, runs in enumerate(_in_column_runs()):
        c0 = chunk * IN_CHUNK
        parts = [jnp.dot(h, w_ref[:, s:s + n], preferred_element_type=F32) for s, n in runs]
        acc = parts[0] if len(parts) == 1 else jnp.concatenate(parts, axis=-1)
        if c0 < ROPE_COLS:
            for hh in range(IN_CHUNK // HEAD_DIM):
                cs = slice(c0 + hh * HEAD_DIM, c0 + (hh + 1) * HEAD_DIM)
                y = _rms(acc[:, hh * HEAD_DIM:(hh + 1) * HEAD_DIM], hg_ref[:, cs])
                partner = jnp.where(low_half, pltpu.roll(y, HEAD_DIM - ROPE_PAIRS, 1),
                                    pltpu.roll(y, ROPE_PAIRS, 1))
                pa_ref[:, cs] = (y * cos + partner * sin).astype(BF16)
            continue
        acc = jnp.concatenate([acc[:, o:o + GROUP_W] * Q_SCALE if _is_query_col(c0 + o) else acc[:, o:o + GROUP_W]
                               for o in range(0, IN_CHUNK, GROUP_W)], axis=-1)
        if c0 < PA_COLS:
            pa_ref[:, c0:c0 + IN_CHUNK] = acc.astype(BF16)
        else:
            pb_ref[:, c0 - PA_COLS:c0 - PA_COLS + IN_CHUNK] = acc


def _in_proj(x2d, gain, w, head_gain, cos_tab, sin_tab, *, seq, tm=512, casts=()):
    t, d = x2d.shape
    assert t % tm == 0 and seq % tm == 0
    assert ROPE_COLS % IN_CHUNK == 0 and PA_COLS % IN_CHUNK == 0 and IN_COLS % IN_CHUNK == 0
    pos_blocks = seq // tm
    in_specs = [
        pl.BlockSpec((tm, d), lambda i: (i, 0)),
        _resident(gain.shape),
        _resident(w.shape),
        _resident(head_gain.shape),
        pl.BlockSpec((tm, HEAD_DIM), lambda i: (i % pos_blocks, 0)),
        pl.BlockSpec((tm, HEAD_DIM), lambda i: (i % pos_blocks, 0)),
    ]
    out_specs = [pl.BlockSpec((tm, PA_COLS), lambda i: (i, 0)),
                 pl.BlockSpec((tm, PB_COLS), lambda i: (i, 0))]
    out_shape = [jax.ShapeDtypeStruct((t, PA_COLS), BF16),
                 jax.ShapeDtypeStruct((t, PB_COLS), F32)]
    operands = [x2d, gain, w, head_gain, cos_tab, sin_tab]
    for cast in casts:
        src_spec, dst_spec, dst_shape = _cast_rider(cast, (t // tm,))
        in_specs.append(src_spec)
        out_specs.append(dst_spec)
        out_shape.append(dst_shape)
        operands.append(cast)
    return pl.pallas_call(
        _in_proj_kernel,
        grid=(t // tm,),
        in_specs=in_specs,
        out_specs=out_specs,
        out_shape=out_shape,
        compiler_params=_params("parallel"),
        name="in_proj",
    )(*operands)


def _norm_matmul_kernel(x_ref, g_ref, w_ref, o_ref, h_scr):
    @pl.when(pl.program_id(1) == 0)
    def _():
        h_scr[...] = _rms(x_ref[...], g_ref[...]).astype(BF16)

    o_ref[...] = jnp.dot(h_scr[...], w_ref[...], preferred_element_type=F32).astype(o_ref.dtype)


def _mem_kv(mem2d, gain, w, *, tm=1024, tn=1024):
    t, d = mem2d.shape
    n = w.shape[1]
    assert t % tm == 0 and n % tn == 0
    return pl.pallas_call(
        _norm_matmul_kernel,
        grid=(t // tm, n // tn),
        in_specs=[
            pl.BlockSpec((tm, d), lambda i, j: (i, 0)),
            pl.BlockSpec((1, d), lambda i, j: (0, 0)),
            pl.BlockSpec((d, tn), lambda i, j: (0, j)),
        ],
        out_specs=pl.BlockSpec((tm, tn), lambda i, j: (i, j)),
        out_shape=jax.ShapeDtypeStruct((t, n), BF16),
        scratch_shapes=[pltpu.VMEM((tm, d), BF16)],
        compiler_params=_params("parallel", "arbitrary"),
        name="mem_kv",
    )(mem2d, gain, w)


def _attn_am_kernel(qa_ref, ka_ref, va_ref, qm_ref, kvm_ref, *rest):
    if len(rest) == 6:
        cast_src_ref, oa_ref, om_ref, cast_dst_ref, va1_scr, vm1_scr = rest
        cast_dst_ref[...] = cast_src_ref[...].astype(BF16)
    else:
        oa_ref, om_ref, va1_scr, vm1_scr = rest

    @pl.when(pl.program_id(1) == 0)
    def _():
        for kv in range(N_KV_A):
            va1_scr[kv, :, :HEAD_DIM] = va_ref[:, kv * HEAD_DIM:(kv + 1) * HEAD_DIM]
            va1_scr[kv, :, HEAD_DIM:] = jnp.ones((va_ref.shape[0], HEAD_DIM), BF16)
        for h in range(N_HEADS_M):
            vm1_scr[h, :, :HEAD_DIM] = kvm_ref[:, WIDTH_M + h * HEAD_DIM:WIDTH_M + (h + 1) * HEAD_DIM]
            vm1_scr[h, :, HEAD_DIM:] = jnp.ones((kvm_ref.shape[0], HEAD_DIM), BF16)

    def head_cols(h):
        return slice(h * HEAD_DIM, (h + 1) * HEAD_DIM)

    units = [(qa_ref, ka_ref, head_cols(h // GQA_GROUP), (va1_scr, h // GQA_GROUP), oa_ref, head_cols(h))
             for h in range(N_HEADS_A)]
    units += [(qm_ref, kvm_ref, head_cols(h), (vm1_scr, h), om_ref, head_cols(h)) for h in range(N_HEADS_M)]

    def scores(unit):
        q_ref, k_ref, kcols, _, _, cs = unit
        return _qk(q_ref[:, cs], k_ref[:, kcols])

    s_next = scores(units[0])
    for idx, unit in enumerate(units):
        s = s_next
        if idx + 1 < len(units):
            s_next = scores(units[idx + 1])
        _, _, _, (v1_scr, vi), o_ref, cs = unit
        m = jnp.max(s, axis=-1, keepdims=True)
        p = jnp.exp2(s - m).astype(BF16)
        ol = jnp.dot(p, v1_scr[vi], preferred_element_type=F32)
        o_ref[:, cs] = (ol[:, :HEAD_DIM] / ol[:, HEAD_DIM:]).astype(o_ref.dtype)


def _attn_am(pa, kvm, *, tq=1024, cast=None):
    b, s, _ = pa.shape
    n_mem = kvm.shape[1]
    assert s % tq == 0 and PA_KA % WIDTH_KV_A == 0 and PA_VA % WIDTH_KV_A == 0
    grid = (b, s // tq)
    in_specs = [
        pl.BlockSpec((None, tq, WIDTH_A), lambda bi, i: (bi, i, PA_QA // WIDTH_A)),
        pl.BlockSpec((None, s, WIDTH_KV_A), lambda bi, i: (bi, 0, PA_KA // WIDTH_KV_A)),
        pl.BlockSpec((None, s, WIDTH_KV_A), lambda bi, i: (bi, 0, PA_VA // WIDTH_KV_A)),
        pl.BlockSpec((None, tq, WIDTH_M), lambda bi, i: (bi, i, PA_QM // WIDTH_M)),
        pl.BlockSpec((None, n_mem, 2 * WIDTH_M), lambda bi, i: (bi, 0, 0)),
    ]
    out_specs = [pl.BlockSpec((None, tq, WIDTH_A), lambda bi, i: (bi, i, 0)),
                 pl.BlockSpec((None, tq, WIDTH_M), lambda bi, i: (bi, i, 0))]
    out_shape = [jax.ShapeDtypeStruct((b, s, WIDTH_A), BF16),
                 jax.ShapeDtypeStruct((b, s, WIDTH_M), BF16)]
    operands = [pa, pa, pa, pa, kvm]
    if cast is not None:
        src_spec, dst_spec, dst_shape = _cast_rider(cast, grid)
        in_specs.append(src_spec)
        out_specs.append(dst_spec)
        out_shape.append(dst_shape)
        operands.append(cast)
    return pl.pallas_call(
        _attn_am_kernel,
        grid=grid,
        in_specs=in_specs,
        out_specs=out_specs,
        out_shape=out_shape,
        scratch_shapes=[pltpu.VMEM((N_KV_A, s, 2 * HEAD_DIM), BF16),
                        pltpu.VMEM((N_HEADS_M, n_mem, 2 * HEAD_DIM), BF16)],
        compiler_params=_params("parallel", "arbitrary"),
        name="attn_am",
    )(*operands)


B_TQ = 128
B_HALF = 64
B_SKEW = 2
assert all(w // (2 * d) == B_HALF for w, d in DIL_PAIRS)


def _band_window(sub_len):
    return min(sub_len, B_TQ + 2 * B_HALF)


def _band_blocks(sub_len):
    w = _band_window(sub_len)
    nblk = sub_len // B_TQ
    out = []
    for qb in range(nblk):
        a = qb * B_TQ
        ks = min(max(a - B_HALF, 0), sub_len - w)
        case = 0 if qb == 0 else (2 if qb == nblk - 1 else 1)
        out.append((a, ks, case))
    return out


def _t5_bucket(rel):
    nb = NUM_BUCKETS // 2
    max_exact = nb // 2
    base = np.where(rel > 0, nb, 0)
    n = np.abs(rel)
    nf = np.maximum(n, 1).astype(np.float64)
    large = max_exact + (np.log(nf / max_exact) / math.log(MAX_DISTANCE / max_exact)
                         * (nb - max_exact)).astype(np.int64)
    large = np.minimum(large, nb - 1)
    return base + np.where(n < max_exact, n, large)


def _band_buckets(sub_len, dil):
    w = _band_window(sub_len)
    offsets = {}
    for a, ks, case in _band_blocks(sub_len):
        assert offsets.setdefault(case, ks - a) == ks - a
    i = np.arange(B_TQ)[:, None]
    j = np.arange(w)[None, :]
    tiles = []
    for case in sorted(offsets):
        rel = offsets[case] + j - i
        tiles.append(np.where(np.abs(rel) <= B_HALF, _t5_bucket(rel * dil), -1))
    return jnp.asarray(np.stack(tiles), jnp.int32)


def _band_bias_kernel(tab_ref, *refs):
    for group, (bucket_ref, o_ref) in enumerate(zip(refs[:N_DIL], refs[N_DIL:])):
        bucket = bucket_ref[...]
        for h in range(HEADS_PER_DIL):
            acc = jnp.full(bucket.shape, NEG_INF, F32)
            for kb in range(NUM_BUCKETS):
                acc = jnp.where(bucket == kb, tab_ref[kb, group * HEADS_PER_DIL + h] * LOG2E, acc)
            o_ref[:, h] = acc


def _band_biases(rel_bias, seq):
    buckets = [_band_buckets(seq // dil, dil) for _, dil in DIL_PAIRS]
    vmem = pl.BlockSpec(memory_space=pltpu.VMEM)
    return pl.pallas_call(
        _band_bias_kernel,
        in_specs=[pl.BlockSpec(memory_space=pltpu.SMEM)] + [vmem] * N_DIL,
        out_specs=[vmem] * N_DIL,
        out_shape=[jax.ShapeDtypeStruct((bk.shape[0], HEADS_PER_DIL) + bk.shape[1:], F32) for bk in buckets],
        name="band_bias",
    )(rel_bias.astype(F32), *buckets)


def _attn_b_kernel(*refs, seq):
    qkv_refs, bias_refs = refs[:3 * N_DIL], refs[3 * N_DIL:4 * N_DIL]
    o_ref, o_scr, lse_scr = refs[4 * N_DIL:]
    tiles = [(g, dil, r, a, ks, case)
             for g, (_, dil) in enumerate(DIL_PAIRS)
             for r in range(dil)
             for a, ks, case in _band_blocks(seq // dil)]

    def rows(dil, r, start, n):
        return pl.ds(start, n) if dil == 1 else pl.ds(r + dil * start, n, stride=dil)

    def load_scores(tile):
        g, dil, r, a, ks, case = tile
        q_ref, k_ref, v_ref = qkv_refs[3 * g:3 * g + 3]
        w = _band_window(seq // dil)
        q = q_ref[rows(dil, r, a, B_TQ), :].astype(BF16)
        k = k_ref[rows(dil, r, ks, w), :].astype(BF16)
        v = v_ref[rows(dil, r, ks, w), :].astype(BF16)
        v1 = jnp.concatenate([v, jnp.ones_like(v)], axis=-1)
        return _qk(q, k) + bias_refs[g][case], v1

    def softmax_pv(s, v1):
        m = jnp.max(s, axis=-1, keepdims=True)
        p = jnp.exp2(s - m).astype(BF16)
        return jnp.dot(p, v1, preferred_element_type=F32), m

    def store(tile, ol, m):
        g, dil, r, a, _, _ = tile
        l = ol[:, HEAD_DIM:]
        o_scr[g, rows(dil, r, a, B_TQ), :] = ol[:, :HEAD_DIM] / l
        lse_scr[g, rows(dil, r, a, B_TQ), :] = m + jnp.log(l) * LOG2E

    n = len(tiles)
    scored, popped = {}, {}
    for step in range(n + 2 * B_SKEW):
        if step < n:
            scored[step] = load_scores(tiles[step])
        t = step - B_SKEW
        if 0 <= t < n:
            popped[t] = softmax_pv(*scored.pop(t))
        t = step - 2 * B_SKEW
        if 0 <= t < n:
            store(tiles[t], *popped.pop(t))
    lse = [lse_scr[g] for g in range(N_DIL)]
    mx = functools.reduce(jnp.maximum, lse)
    e = [jnp.exp2(x - mx) for x in lse]
    den = functools.reduce(lambda a, b: a + b, e)
    for g in range(N_DIL):
        o_ref[:, g * HEAD_DIM:(g + 1) * HEAD_DIM] = ((e[g] / den) * o_scr[g]).astype(o_ref.dtype)


def _attn_b(pa, pb, biases):
    b, s, _ = pa.shape

    def col(cols0, part, g):
        blk = (cols0 + (3 * g + part) * GROUP_W) // HEAD_DIM
        return pl.BlockSpec((None, s, HEAD_DIM), lambda bi, h: (bi, 0, blk + h))

    in_specs = [col(PA_B0, part, 0) for part in range(3)]
    operands = [pa] * 3
    for g in range(1, N_DIL):
        in_specs += [col(0, part, g - 1) for part in range(3)]
        operands += [pb] * 3
    for bias in biases:
        ncase, _, tq, w = bias.shape
        in_specs.append(pl.BlockSpec((ncase, None, tq, w), lambda bi, h: (0, h, 0, 0)))
    return pl.pallas_call(
        functools.partial(_attn_b_kernel, seq=s),
        grid=(b, HEADS_PER_DIL),
        in_specs=in_specs,
        out_specs=pl.BlockSpec((None, s, N_DIL * HEAD_DIM), lambda bi, h: (bi, 0, h)),
        out_shape=jax.ShapeDtypeStruct((b, s, WIDTH_B), BF16),
        scratch_shapes=[pltpu.VMEM((N_DIL, s, HEAD_DIM), F32), pltpu.VMEM((N_DIL, s, HEAD_DIM), F32)],
        compiler_params=_params("parallel", "parallel"),
        name="attn_b",
    )(*operands, *biases)


def _out_proj_kernel(oa_ref, ob_ref, om_ref, x_ref, ga_ref, gb_ref, gm_ref, w_ref, gpost_ref, *rest):
    if len(rest) == 3:
        cast_src_ref, o_ref, cast_dst_ref = rest
        cast_dst_ref[...] = cast_src_ref[...].astype(BF16)
    else:
        o_ref, = rest
    ob = jnp.concatenate([ob_ref[:, (h * N_DIL + g) * HEAD_DIM:(h * N_DIL + g + 1) * HEAD_DIM]
                          for g in range(N_DIL) for h in range(HEADS_PER_DIL)], axis=-1)
    mix = jnp.concatenate([_rms(oa_ref[...].astype(F32), ga_ref[...]).astype(BF16),
                           _rms(ob.astype(F32), gb_ref[...]).astype(BF16),
                           _rms(om_ref[...].astype(F32), gm_ref[...]).astype(BF16)], axis=-1)
    y = jnp.dot(mix, w_ref[...], preferred_element_type=F32)
    o_ref[...] = x_ref[...] + _rms(y, gpost_ref[...])


def _out_proj(oa, ob, om, x2d, ga, gb, gm, w, gpost, *, tm=512, cast=None):
    t, d = x2d.shape
    assert t % tm == 0

    def rows(width):
        return pl.BlockSpec((tm, width), lambda i: (i, 0))

    in_specs = [rows(WIDTH_A), rows(WIDTH_B), rows(WIDTH_M), rows(d),
                _resident(ga.shape), _resident(gb.shape), _resident(gm.shape),
                _resident(w.shape), _resident(gpost.shape)]
    out_specs = [rows(d)]
    out_shape = [jax.ShapeDtypeStruct((t, d), F32)]
    operands = [oa, ob, om, x2d, ga, gb, gm, w, gpost]
    if cast is not None:
        src_spec, dst_spec, dst_shape = _cast_rider(cast, (t // tm,))
        in_specs.append(src_spec)
        out_specs.append(dst_spec)
        out_shape.append(dst_shape)
        operands.append(cast)
    out = pl.pallas_call(
        _out_proj_kernel,
        grid=(t // tm,),
        in_specs=in_specs,
        out_specs=out_specs,
        out_shape=out_shape,
        compiler_params=_params("parallel"),
        name="out_proj",
    )(*operands)
    return out[0] if cast is None else tuple(out)


def _ffn_kernel(x_ref, gpre_ref, wup_ref, wdn_ref, gpost_ref, o_ref, h_scr):
    j = pl.program_id(1)
    last = pl.num_programs(1) - 1
    tm = x_ref.shape[0]

    def step(first, final):
        for rs in (_row_halves(tm) if first or final else (slice(0, tm),)):
            if first:
                h = _rms(x_ref[rs, :], gpre_ref[...]).astype(BF16)
                h_scr[rs, :] = h
            else:
                h = h_scr[rs, :]
            u = jnp.dot(h, wup_ref[...], preferred_element_type=F32)
            u = jnp.square(jnp.maximum(u, 0.0)).astype(BF16)
            acc = jnp.dot(u, wdn_ref[...], preferred_element_type=F32)
            if not first:
                acc = o_ref[rs, :] + acc
            if final:
                acc = x_ref[rs, :] + _rms(acc, gpost_ref[...])
            o_ref[rs, :] = acc

    pl.when(j == 0)(functools.partial(step, True, False))
    pl.when((j > 0) & (j < last))(functools.partial(step, False, False))
    pl.when(j == last)(functools.partial(step, False, True))


def _ffn(x2d, gpre, w_up, w_down, gpost, *, tm=1024, tf=FFN_CHUNK):
    t, d = x2d.shape
    d_ff = w_up.shape[1]
    assert t % tm == 0 and d_ff % tf == 0 and d_ff // tf >= 2
    return pl.pallas_call(
        _ffn_kernel,
        grid=(t // tm, d_ff // tf),
        in_specs=[
            pl.BlockSpec((tm, d), lambda i, j: (i, 0)),
            _resident(gpre.shape),
            pl.BlockSpec((d, tf), lambda i, j: (0, j)),
            pl.BlockSpec((tf, d), lambda i, j: (j, 0)),
            _resident(gpost.shape),
        ],
        out_specs=pl.BlockSpec((tm, d), lambda i, j: (i, 0)),
        out_shape=jax.ShapeDtypeStruct((t, d), F32),
        scratch_shapes=[pltpu.VMEM((tm, d), BF16)],
        compiler_params=_params("parallel", "arbitrary"),
        name="ffn",
    )(x2d, gpre, w_up, w_down, gpost)


def _rope_tables(seq_len):
    rows = seq_len // GRID_W
    row = np.repeat(np.arange(rows), GRID_W).astype(np.float64)
    col = np.tile(np.arange(GRID_W), rows).astype(np.float64)
    inv = ROPE_THETA ** (-(2.0 * np.arange(ROPE_PAIRS)) / ROPE_AXIS_DIM)
    ang_r = row[:, None] * inv
    ang_c = col[:, None] * inv
    cos = np.concatenate([np.cos(ang_r)] * 2 + [np.cos(ang_c)] * 2, axis=-1)
    sin = np.concatenate([-np.sin(ang_r), np.sin(ang_r), -np.sin(ang_c), np.sin(ang_c)], axis=-1)
    return jnp.asarray(cos, F32), jnp.asarray(sin, F32)


def _row(v):
    return v.reshape(1, -1).astype(F32)


def _encoder_layer(x, mem, p, biases, late_w):
    b, s, d = x.shape
    x2d = x.reshape(b * s, d)
    cast = late_w["w_up"].dtype != BF16
    cos_tab, sin_tab = _rope_tables(s)
    proj = _in_proj(x2d, p["pre_mix_norm"], p["w_in"], p["head_gain"], cos_tab, sin_tab, seq=s,
                    casts=(late_w["w_up"], late_w["w_mem_kv"]) if cast else ())
    pa = proj[0].reshape(b, s, PA_COLS)
    pb = proj[1].reshape(b, s, PB_COLS)
    w_mem_kv = proj[3] if cast else late_w["w_mem_kv"]
    kv = _mem_kv(mem.reshape(-1, d), p["mem_norm"], w_mem_kv).reshape(b, mem.shape[1], -1)

    attn = _attn_am(pa, kv, cast=late_w["w_out"] if cast else None)
    oa, om = attn[0], attn[1]
    ob = _attn_b(pa, pb, biases)
    w_out = attn[2] if cast else late_w["w_out"]

    x1 = _out_proj(oa.reshape(b * s, WIDTH_A), ob.reshape(b * s, WIDTH_B), om.reshape(b * s, WIDTH_M), x2d,
                   p["out_norm_a"], p["out_norm_b"], p["out_norm_m"], w_out, p["post_mix_norm"],
                   cast=late_w["w_down"] if cast else None)
    if cast:
        x1, w_down = x1
        late_w = {"w_mem_kv": w_mem_kv, "w_out": w_out, "w_up": proj[2], "w_down": w_down}
    y = _ffn(x1, p["pre_ffn_norm"], late_w["w_up"], late_w["w_down"], p["post_ffn_norm"])
    return y.reshape(b, s, d), late_w


def kernel(x_prompt, x_sample, mem_prompt, mem_sample, rel_bias, pre_mix_norm, w_in, q_norm_a, k_norm_a, mem_norm, w_mem_kv, out_norm_a, out_norm_b, out_norm_m, w_out, post_mix_norm, pre_ffn_norm, w_up, w_down, post_ffn_norm):
    depth = w_in.shape[0]
    assert x_prompt.shape[1] == x_sample.shape[1]
    seq = x_prompt.shape[1]
    biases = _band_biases(rel_bias, seq)
    layers = []
    for l in range(depth):
        layers.append({
            "pre_mix_norm": _row(pre_mix_norm[l]),
            "w_in": w_in[l].astype(BF16),
            "head_gain": _row(jnp.concatenate([jnp.tile(q_norm_a[l], N_HEADS_A) * Q_SCALE,
                                               jnp.tile(k_norm_a[l], N_KV_A)])),
            "mem_norm": _row(mem_norm[l]),
            "out_norm_a": _row(out_norm_a[l]),
            "out_norm_b": _row(out_norm_b[l]),
            "out_norm_m": _row(out_norm_m[l]),
            "post_mix_norm": _row(post_mix_norm[l]),
            "pre_ffn_norm": _row(pre_ffn_norm[l]),
            "post_ffn_norm": _row(post_ffn_norm[l]),
        })

    xp, xs = x_prompt, x_sample
    for l, p in enumerate(layers):
        late_w = {"w_mem_kv": w_mem_kv[l], "w_out": w_out[l], "w_up": w_up[l], "w_down": w_down[l]}
        xp, late_w = _encoder_layer(xp, mem_prompt, p, biases, late_w)
        xs, _ = _encoder_layer(xs, mem_sample, p, biases, late_w)
    return (xp, xs)
```

```python
import functools
import math

import jax
import jax.numpy as jnp
import numpy as np
from jax import lax
from jax.experimental import pallas as pl
from jax.experimental.pallas import tpu as pltpu

F32 = jnp.float32
BF16 = jnp.bfloat16

HEAD_DIM = 128
N_HEADS_A = 6
N_KV_A = 2
GQA_GROUP = N_HEADS_A // N_KV_A
DIL_PAIRS = ((128, 1), (512, 4), (2048, 16))
N_DIL = len(DIL_PAIRS)
HEADS_PER_DIL = 2
N_HEADS_B = N_DIL * HEADS_PER_DIL
N_HEADS_M = 4
WIDTH_A = N_HEADS_A * HEAD_DIM
WIDTH_KV_A = N_KV_A * HEAD_DIM
WIDTH_B = N_HEADS_B * HEAD_DIM
WIDTH_M = N_HEADS_M * HEAD_DIM
GROUP_W = HEADS_PER_DIL * HEAD_DIM
GRID_W = 64
ROPE_THETA = 10000.0
ROPE_AXIS_DIM = HEAD_DIM // 2
ROPE_PAIRS = ROPE_AXIS_DIM // 2
NUM_BUCKETS = 32
MAX_DISTANCE = 1024
EPS = 1e-6
NEG_INF = -1e30
ATTN_SCALE = HEAD_DIM ** -0.5
LOG2E = math.log2(math.e)

PA_QA = 0
PA_KA = PA_QA + WIDTH_A
ROPE_COLS = PA_KA + WIDTH_KV_A
PA_QM = ROPE_COLS
PA_VA = PA_QM + WIDTH_M
PA_B0 = PA_VA + WIDTH_KV_A
PA_COLS = PA_B0 + 3 * GROUP_W
PB_COLS = (N_DIL - 1) * 3 * GROUP_W
assert PA_QA % (GQA_GROUP * HEAD_DIM) == 0 and PA_QM % WIDTH_M == 0
IN_COLS = PA_COLS + PB_COLS
IN_CHUNK = 512
Q_SCALE = ATTN_SCALE * LOG2E


def _is_query_col(col):
    if PA_QM <= col < PA_QM + WIDTH_M or col == PA_B0:
        return True
    return col >= PA_COLS and (col - PA_COLS) % (3 * GROUP_W) == 0

FFN_CHUNK = 1024

BF16_SUBLANES = 16
V7X_VMEM_BYTES = 64 * 1024 * 1024
VMEM_LIMIT_BYTES = V7X_VMEM_BYTES - 4 * 1024 * 1024


def _params(*semantics, vmem_limit_bytes=VMEM_LIMIT_BYTES):
    return pltpu.CompilerParams(dimension_semantics=semantics, vmem_limit_bytes=vmem_limit_bytes)


def _resident(shape):
    zeros = (0,) * len(shape)
    return pl.BlockSpec(shape, lambda *_: zeros, pipeline_mode=pl.Buffered(1))


def _rms(v, g):
    ms = jnp.mean(v * v, axis=-1, keepdims=True)
    return v * lax.rsqrt(ms + EPS) * g


def _row_halves(rows):
    return (slice(0, rows // 2), slice(rows // 2, rows))


def _qk(q, k):
    return lax.dot_general(q, k, (((1,), (1,)), ((), ())), preferred_element_type=F32)


def _cast_rider(weight, grid):
    rows, cols = weight.shape
    n_steps = math.prod(grid)
    assert rows % (n_steps * BF16_SUBLANES) == 0

    def slab(*idx):
        flat = 0
        for i, n in zip(idx, grid):
            flat = flat * n + i
        return (flat, 0)

    spec = pl.BlockSpec((rows // n_steps, cols), slab)
    return spec, spec, jax.ShapeDtypeStruct((rows, cols), BF16)


def _in_column_runs():
    starts = np.cumsum([0, WIDTH_A, WIDTH_KV_A, WIDTH_KV_A, WIDTH_B, WIDTH_B, WIDTH_B])
    qa, ka, va, qb, kb, vb, qm = (int(c) for c in starts)
    segs = [(qa, WIDTH_A), (ka, WIDTH_KV_A), (qm, WIDTH_M), (va, WIDTH_KV_A)]
    for g in range(N_DIL):
        segs += [(part + g * GROUP_W, GROUP_W) for part in (qb, kb, vb)]
    cols = np.concatenate([np.arange(s, s + w) for s, w in segs])
    assert cols.size == IN_COLS and np.array_equal(np.sort(cols), np.arange(IN_COLS))
    chunks = []
    for c0 in range(0, IN_COLS, IN_CHUNK):
        piece = cols[c0:c0 + IN_CHUNK]
        breaks = [0] + [int(i) + 1 for i in np.nonzero(np.diff(piece) != 1)[0]] + [IN_CHUNK]
        chunks.append([(int(piece[a]), b - a) for a, b in zip(breaks[:-1], breaks[1:])])
    return chunks


def _in_proj_kernel(x_ref, g_ref, w_ref, hg_ref, cos_ref, sin_ref, *rest):
    n_cast = (len(rest) - 2) // 2
    pa_ref, pb_ref = rest[n_cast:n_cast + 2]
    for src_ref, dst_ref in zip(rest[:n_cast], rest[n_cast + 2:]):
        dst_ref[...] = src_ref[...].astype(BF16)
    h = _rms(x_ref[...], g_ref[...]).astype(BF16)
    tm = h.shape[0]
    cos = cos_ref[...]
    sin = sin_ref[...]
    lane = lax.broadcasted_iota(jnp.int32, (tm, HEAD_DIM), 1)
    low_half = (lane % ROPE_AXIS_DIM) < ROPE_PAIRS
    for chunk, runs in enumerate(_in_column_runs()):
        c0 = chunk * IN_CHUNK
        parts = [jnp.dot(h, w_ref[:, s:s + n], preferred_element_type=F32) for s, n in runs]
        acc = parts[0] if len(parts) == 1 else jnp.concatenate(parts, axis=-1)
        if c0 < ROPE_COLS:
            for hh in range(IN_CHUNK // HEAD_DIM):
                cs = slice(c0 + hh * HEAD_DIM, c0 + (hh + 1) * HEAD_DIM)
                y = _rms(acc[:, hh * HEAD_DIM:(hh + 1) * HEAD_DIM], hg_ref[:, cs])
                partner = jnp.where(low_half, pltpu.roll(y, HEAD_DIM - ROPE_PAIRS, 1),
                                    pltpu.roll(y, ROPE_PAIRS, 1))
                pa_ref[:, cs] = (y * cos + partner * sin).astype(BF16)
            continue
        acc = jnp.concatenate([acc[:, o:o + GROUP_W] * Q_SCALE if _is_query_col(c0 + o) else acc[:, o:o + GROUP_W]
                               for o in range(0, IN_CHUNK, GROUP_W)], axis=-1)
        if c0 < PA_COLS:
            pa_ref[:, c0:c0 + IN_CHUNK] = acc.astype(BF16)
        else:
            pb_ref[:, c0 - PA_COLS:c0 - PA_COLS + IN_CHUNK] = acc


def _in_proj(x2d, gain, w, head_gain, cos_tab, sin_tab, *, seq, tm=512, casts=()):
    t, d = x2d.shape
    assert t % tm == 0 and seq % tm == 0
    assert ROPE_COLS % IN_CHUNK == 0 and PA_COLS % IN_CHUNK == 0 and IN_COLS % IN_CHUNK == 0
    pos_blocks = seq // tm
    in_specs = [
        pl.BlockSpec((tm, d), lambda i: (i, 0)),
        _resident(gain.shape),
        _resident(w.shape),
        _resident(head_gain.shape),
        pl.BlockSpec((tm, HEAD_DIM), lambda i: (i % pos_blocks, 0)),
        pl.BlockSpec((tm, HEAD_DIM), lambda i: (i % pos_blocks, 0)),
    ]
    out_specs = [pl.BlockSpec((tm, PA_COLS), lambda i: (i, 0)),
                 pl.BlockSpec((tm, PB_COLS), lambda i: (i, 0))]
    out_shape = [jax.ShapeDtypeStruct((t, PA_COLS), BF16),
                 jax.ShapeDtypeStruct((t, PB_COLS), F32)]
    operands = [x2d, gain, w, head_gain, cos_tab, sin_tab]
    for cast in casts:
        src_spec, dst_spec, dst_shape = _cast_rider(cast, (t // tm,))
        in_specs.append(src_spec)
        out_specs.append(dst_spec)
        out_shape.append(dst_shape)
        operands.append(cast)
    return pl.pallas_call(
        _in_proj_kernel,
        grid=(t // tm,),
        in_specs=in_specs,
        out_specs=out_specs,
        out_shape=out_shape,
        compiler_params=_params("parallel"),
        name="in_proj",
    )(*operands)


def _norm_matmul_kernel(x_ref, g_ref, w_ref, o_ref, h_scr):
    @pl.when(pl.program_id(1) == 0)
    def _():
        h_scr[...] = _rms(x_ref[...], g_ref[...]).astype(BF16)

    o_ref[...] = jnp.dot(h_scr[...], w_ref[...], preferred_element_type=F32).astype(o_ref.dtype)


def _mem_kv(mem2d, gain, w, *, tm=512, tn=1024):
    t, d = mem2d.shape
    n = w.shape[1]
    assert t % tm == 0 and n % tn == 0
    return pl.pallas_call(
        _norm_matmul_kernel,
        grid=(t // tm, n // tn),
        in_specs=[
            pl.BlockSpec((tm, d), lambda i, j: (i, 0)),
            pl.BlockSpec((1, d), lambda i, j: (0, 0)),
            pl.BlockSpec((d, tn), lambda i, j: (0, j)),
        ],
        out_specs=pl.BlockSpec((tm, tn), lambda i, j: (i, j)),
        out_shape=jax.ShapeDtypeStruct((t, n), BF16),
        scratch_shapes=[pltpu.VMEM((tm, d), BF16)],
        compiler_params=_params("parallel", "arbitrary"),
        name="mem_kv",
    )(mem2d, gain, w)


def _attn_am_kernel(qa_ref, ka_ref, va_ref, qm_ref, kvm_ref, *rest):
    if len(rest) == 6:
        cast_src_ref, oa_ref, om_ref, cast_dst_ref, va1_scr, vm1_scr = rest
        cast_dst_ref[...] = cast_src_ref[...].astype(BF16)
    else:
        oa_ref, om_ref, va1_scr, vm1_scr = rest

    @pl.when(pl.program_id(1) == 0)
    def _():
        for kv in range(N_KV_A):
            va1_scr[kv, :, :HEAD_DIM] = va_ref[:, kv * HEAD_DIM:(kv + 1) * HEAD_DIM]
            va1_scr[kv, :, HEAD_DIM:] = jnp.ones((va_ref.shape[0], HEAD_DIM), BF16)
        for h in range(N_HEADS_M):
            vm1_scr[h, :, :HEAD_DIM] = kvm_ref[:, WIDTH_M + h * HEAD_DIM:WIDTH_M + (h + 1) * HEAD_DIM]
            vm1_scr[h, :, HEAD_DIM:] = jnp.ones((kvm_ref.shape[0], HEAD_DIM), BF16)

    def head_cols(h):
        return slice(h * HEAD_DIM, (h + 1) * HEAD_DIM)

    units = [(qa_ref, ka_ref, head_cols(h // GQA_GROUP), (va1_scr, h // GQA_GROUP), oa_ref, head_cols(h))
             for h in range(N_HEADS_A)]
    units += [(qm_ref, kvm_ref, head_cols(h), (vm1_scr, h), om_ref, head_cols(h)) for h in range(N_HEADS_M)]

    def scores(unit):
        q_ref, k_ref, kcols, _, _, cs = unit
        return _qk(q_ref[:, cs], k_ref[:, kcols])

    s_next = scores(units[0])
    for idx, unit in enumerate(units):
        s = s_next
        if idx + 1 < len(units):
            s_next = scores(units[idx + 1])
        _, _, _, (v1_scr, vi), o_ref, cs = unit
        m = jnp.max(s, axis=-1, keepdims=True)
        p = jnp.exp2(s - m).astype(BF16)
        ol = jnp.dot(p, v1_scr[vi], preferred_element_type=F32)
        o_ref[:, cs] = (ol[:, :HEAD_DIM] / ol[:, HEAD_DIM:]).astype(o_ref.dtype)


def _attn_am(pa, kvm, *, tq=1024, cast=None):
    b, s, _ = pa.shape
    n_mem = kvm.shape[1]
    assert s % tq == 0 and PA_KA % WIDTH_KV_A == 0 and PA_VA % WIDTH_KV_A == 0
    grid = (b, s // tq)
    in_specs = [
        pl.BlockSpec((None, tq, WIDTH_A), lambda bi, i: (bi, i, PA_QA // WIDTH_A)),
        pl.BlockSpec((None, s, WIDTH_KV_A), lambda bi, i: (bi, 0, PA_KA // WIDTH_KV_A)),
        pl.BlockSpec((None, s, WIDTH_KV_A), lambda bi, i: (bi, 0, PA_VA // WIDTH_KV_A)),
        pl.BlockSpec((None, tq, WIDTH_M), lambda bi, i: (bi, i, PA_QM // WIDTH_M)),
        pl.BlockSpec((None, n_mem, 2 * WIDTH_M), lambda bi, i: (bi, 0, 0)),
    ]
    out_specs = [pl.BlockSpec((None, tq, WIDTH_A), lambda bi, i: (bi, i, 0)),
                 pl.BlockSpec((None, tq, WIDTH_M), lambda bi, i: (bi, i, 0))]
    out_shape = [jax.ShapeDtypeStruct((b, s, WIDTH_A), BF16),
                 jax.ShapeDtypeStruct((b, s, WIDTH_M), BF16)]
    operands = [pa, pa, pa, pa, kvm]
    if cast is not None:
        src_spec, dst_spec, dst_shape = _cast_rider(cast, grid)
        in_specs.append(src_spec)
        out_specs.append(dst_spec)
        out_shape.append(dst_shape)
        operands.append(cast)
    return pl.pallas_call(
        _attn_am_kernel,
        grid=grid,
        in_specs=in_specs,
        out_specs=out_specs,
        out_shape=out_shape,
        scratch_shapes=[pltpu.VMEM((N_KV_A, s, 2 * HEAD_DIM), BF16),
                        pltpu.VMEM((N_HEADS_M, n_mem, 2 * HEAD_DIM), BF16)],
        compiler_params=_params("parallel", "arbitrary"),
        name="attn_am",
    )(*operands)


B_TQ = 128
B_HALF = 64
B_SKEW = 2
assert all(w // (2 * d) == B_HALF for w, d in DIL_PAIRS)


def _band_window(sub_len):
    return min(sub_len, B_TQ + 2 * B_HALF)


def _band_blocks(sub_len):
    w = _band_window(sub_len)
    nblk = sub_len // B_TQ
    out = []
    for qb in range(nblk):
        a = qb * B_TQ
        ks = min(max(a - B_HALF, 0), sub_len - w)
        case = 0 if qb == 0 else (2 if qb == nblk - 1 else 1)
        out.append((a, ks, case))
    return out


def _t5_bucket(rel):
    nb = NUM_BUCKETS // 2
    max_exact = nb // 2
    base = np.where(rel > 0, nb, 0)
    n = np.abs(rel)
    nf = np.maximum(n, 1).astype(np.float64)
    large = max_exact + (np.log(nf / max_exact) / math.log(MAX_DISTANCE / max_exact)
                         * (nb - max_exact)).astype(np.int64)
    large = np.minimum(large, nb - 1)
    return base + np.where(n < max_exact, n, large)


def _band_buckets(sub_len, dil):
    w = _band_window(sub_len)
    offsets = {}
    for a, ks, case in _band_blocks(sub_len):
        assert offsets.setdefault(case, ks - a) == ks - a
    i = np.arange(B_TQ)[:, None]
    j = np.arange(w)[None, :]
    tiles = []
    for case in sorted(offsets):
        rel = offsets[case] + j - i
        tiles.append(np.where(np.abs(rel) <= B_HALF, _t5_bucket(rel * dil), -1))
    return jnp.asarray(np.stack(tiles), jnp.int32)


def _band_bias_kernel(tab_ref, *refs):
    for group, (bucket_ref, o_ref) in enumerate(zip(refs[:N_DIL], refs[N_DIL:])):
        bucket = bucket_ref[...]
        for h in range(HEADS_PER_DIL):
            acc = jnp.full(bucket.shape, NEG_INF, F32)
            for kb in range(NUM_BUCKETS):
                acc = jnp.where(bucket == kb, tab_ref[kb, group * HEADS_PER_DIL + h] * LOG2E, acc)
            o_ref[:, h] = acc


def _band_biases(rel_bias, seq):
    buckets = [_band_buckets(seq // dil, dil) for _, dil in DIL_PAIRS]
    vmem = pl.BlockSpec(memory_space=pltpu.VMEM)
    return pl.pallas_call(
        _band_bias_kernel,
        in_specs=[pl.BlockSpec(memory_space=pltpu.SMEM)] + [vmem] * N_DIL,
        out_specs=[vmem] * N_DIL,
        out_shape=[jax.ShapeDtypeStruct((bk.shape[0], HEADS_PER_DIL) + bk.shape[1:], F32) for bk in buckets],
        name="band_bias",
    )(rel_bias.astype(F32), *buckets)


def _attn_b_kernel(*refs, seq):
    qkv_refs, bias_refs = refs[:3 * N_DIL], refs[3 * N_DIL:4 * N_DIL]
    o_ref, o_scr, lse_scr = refs[4 * N_DIL:]
    tiles = [(g, dil, r, a, ks, case)
             for g, (_, dil) in enumerate(DIL_PAIRS)
             for r in range(dil)
             for a, ks, case in _band_blocks(seq // dil)]

    def rows(dil, r, start, n):
        return pl.ds(start, n) if dil == 1 else pl.ds(r + dil * start, n, stride=dil)

    def load_scores(tile):
        g, dil, r, a, ks, case = tile
        q_ref, k_ref, v_ref = qkv_refs[3 * g:3 * g + 3]
        w = _band_window(seq // dil)
        q = q_ref[rows(dil, r, a, B_TQ), :].astype(BF16)
        k = k_ref[rows(dil, r, ks, w), :].astype(BF16)
        v = v_ref[rows(dil, r, ks, w), :].astype(BF16)
        v1 = jnp.concatenate([v, jnp.ones_like(v)], axis=-1)
        return _qk(q, k) + bias_refs[g][case], v1

    def softmax_pv(s, v1):
        m = jnp.max(s, axis=-1, keepdims=True)
        p = jnp.exp2(s - m).astype(BF16)
        return jnp.dot(p, v1, preferred_element_type=F32), m

    def store(tile, ol, m):
        g, dil, r, a, _, _ = tile
        l = ol[:, HEAD_DIM:]
        o_scr[g, rows(dil, r, a, B_TQ), :] = ol[:, :HEAD_DIM] / l
        lse_scr[g, rows(dil, r, a, B_TQ), :] = m + jnp.log(l) * LOG2E

    n = len(tiles)
    scored, popped = {}, {}
    for step in range(n + 2 * B_SKEW):
        if step < n:
            scored[step] = load_scores(tiles[step])
        t = step - B_SKEW
        if 0 <= t < n:
            popped[t] = softmax_pv(*scored.pop(t))
        t = step - 2 * B_SKEW
        if 0 <= t < n:
            store(tiles[t], *popped.pop(t))
    lse = [lse_scr[g] for g in range(N_DIL)]
    mx = functools.reduce(jnp.maximum, lse)
    e = [jnp.exp2(x - mx) for x in lse]
    den = functools.reduce(lambda a, b: a + b, e)
    for g in range(N_DIL):
        o_ref[:, g * HEAD_DIM:(g + 1) * HEAD_DIM] = ((e[g] / den) * o_scr[g]).astype(o_ref.dtype)


def _attn_b(pa, pb, biases):
    b, s, _ = pa.shape

    def col(cols0, part, g):
        blk = (cols0 + (3 * g + part) * GROUP_W) // HEAD_DIM
        return pl.BlockSpec((None, s, HEAD_DIM), lambda bi, h: (bi, 0, blk + h))

    in_specs = [col(PA_B0, part, 0) for part in range(3)]
    operands = [pa] * 3
    for g in range(1, N_DIL):
        in_specs += [col(0, part, g - 1) for part in range(3)]
        operands += [pb] * 3
    for bias in biases:
        ncase, _, tq, w = bias.shape
        in_specs.append(pl.BlockSpec((ncase, None, tq, w), lambda bi, h: (0, h, 0, 0)))
    return pl.pallas_call(
        functools.partial(_attn_b_kernel, seq=s),
        grid=(b, HEADS_PER_DIL),
        in_specs=in_specs,
        out_specs=pl.BlockSpec((None, s, N_DIL * HEAD_DIM), lambda bi, h: (bi, 0, h)),
        out_shape=jax.ShapeDtypeStruct((b, s, WIDTH_B), BF16),
        scratch_shapes=[pltpu.VMEM((N_DIL, s, HEAD_DIM), F32), pltpu.VMEM((N_DIL, s, HEAD_DIM), F32)],
        compiler_params=_params("parallel", "parallel"),
        name="attn_b",
    )(*operands, *biases)


def _out_proj_kernel(oa_ref, ob_ref, om_ref, x_ref, ga_ref, gb_ref, gm_ref, w_ref, gpost_ref, *rest):
    if len(rest) == 3:
        cast_src_ref, o_ref, cast_dst_ref = rest
        cast_dst_ref[...] = cast_src_ref[...].astype(BF16)
    else:
        o_ref, = rest
    ob = jnp.concatenate([ob_ref[:, (h * N_DIL + g) * HEAD_DIM:(h * N_DIL + g + 1) * HEAD_DIM]
                          for g in range(N_DIL) for h in range(HEADS_PER_DIL)], axis=-1)
    mix = jnp.concatenate([_rms(oa_ref[...].astype(F32), ga_ref[...]).astype(BF16),
                           _rms(ob.astype(F32), gb_ref[...]).astype(BF16),
                           _rms(om_ref[...].astype(F32), gm_ref[...]).astype(BF16)], axis=-1)
    y = jnp.dot(mix, w_ref[...], preferred_element_type=F32)
    o_ref[...] = x_ref[...] + _rms(y, gpost_ref[...])


def _out_proj(oa, ob, om, x2d, ga, gb, gm, w, gpost, *, tm=512, cast=None):
    t, d = x2d.shape
    assert t % tm == 0

    def rows(width):
        return pl.BlockSpec((tm, width), lambda i: (i, 0))

    in_specs = [rows(WIDTH_A), rows(WIDTH_B), rows(WIDTH_M), rows(d),
                _resident(ga.shape), _resident(gb.shape), _resident(gm.shape),
                _resident(w.shape), _resident(gpost.shape)]
    out_specs = [rows(d)]
    out_shape = [jax.ShapeDtypeStruct((t, d), F32)]
    operands = [oa, ob, om, x2d, ga, gb, gm, w, gpost]
    if cast is not None:
        src_spec, dst_spec, dst_shape = _cast_rider(cast, (t // tm,))
        in_specs.append(src_spec)
        out_specs.append(dst_spec)
        out_shape.append(dst_shape)
        operands.append(cast)
    out = pl.pallas_call(
        _out_proj_kernel,
        grid=(t // tm,),
        in_specs=in_specs,
        out_specs=out_specs,
        out_shape=out_shape,
        compiler_params=_params("parallel"),
        name="out_proj",
    )(*operands)
    return out[0] if cast is None else tuple(out)


def _ffn_kernel(x_ref, gpre_ref, wup_hbm, wdn_hbm, gpost_ref, o_ref, h_scr, wup_buf, wdn_buf, sem, *, tf, n_chunks):
    i = pl.program_id(0)
    tm = x_ref.shape[0]

    def copies(j, slot):
        col = j * tf if isinstance(j, int) else pl.multiple_of(j * tf, tf)
        return (pltpu.make_async_copy(wup_hbm.at[:, pl.ds(col, tf)], wup_buf.at[slot], sem.at[slot, 0]),
                pltpu.make_async_copy(wdn_hbm.at[pl.ds(col, tf), :], wdn_buf.at[slot], sem.at[slot, 1]))

    def start(j, slot):
        for c in copies(j, slot):
            c.start()

    def wait(j, slot):
        for c in copies(j, slot):
            c.wait()

    pl.when(i == 0)(functools.partial(start, 0, 0))

    def chunk(j, slot, first=False, final=False):
        wait(j, slot)
        if final:
            pl.when(i + 1 < pl.num_programs(0))(functools.partial(start, 0, 1 - slot))
        else:
            start(j + 1, 1 - slot)
        for rs in (_row_halves(tm) if first or final else (slice(0, tm),)):
            if first:
                h = _rms(x_ref[rs, :], gpre_ref[...]).astype(BF16)
                h_scr[rs, :] = h
            else:
                h = h_scr[rs, :]
            u = jnp.dot(h, wup_buf[slot], preferred_element_type=F32)
            u = jnp.square(jnp.maximum(u, 0.0)).astype(BF16)
            acc = jnp.dot(u, wdn_buf[slot], preferred_element_type=F32)
            if not first:
                acc = o_ref[rs, :] + acc
            if final:
                acc = x_ref[rs, :] + _rms(acc, gpost_ref[...])
            o_ref[rs, :] = acc

    chunk(0, 0, first=True)

    def pair(k, carry):
        chunk(1 + 2 * k, 1)
        chunk(2 + 2 * k, 0)
        return carry

    lax.fori_loop(0, (n_chunks - 2) // 2, pair, 0)
    chunk(n_chunks - 1, 1, final=True)


def _ffn(x2d, gpre, w_up, w_down, gpost, *, tm=1024, tf=FFN_CHUNK):
    t, d = x2d.shape
    d_ff = w_up.shape[1]
    n_chunks = d_ff // tf
    assert t % tm == 0 and d_ff % tf == 0 and n_chunks >= 2 and n_chunks % 2 == 0
    return pl.pallas_call(
        functools.partial(_ffn_kernel, tf=tf, n_chunks=n_chunks),
        grid=(t // tm,),
        in_specs=[
            pl.BlockSpec((tm, d), lambda i: (i, 0)),
            _resident(gpre.shape),
            pl.BlockSpec(memory_space=pl.ANY),
            pl.BlockSpec(memory_space=pl.ANY),
            _resident(gpost.shape),
        ],
        out_specs=pl.BlockSpec((tm, d), lambda i: (i, 0)),
        out_shape=jax.ShapeDtypeStruct((t, d), F32),
        scratch_shapes=[pltpu.VMEM((tm, d), BF16),
                        pltpu.VMEM((2, d, tf), BF16),
                        pltpu.VMEM((2, tf, d), BF16),
                        pltpu.SemaphoreType.DMA((2, 2))],
        compiler_params=_params("arbitrary", vmem_limit_bytes=V7X_VMEM_BYTES - 2 * 1024 * 1024),
        name="ffn",
    )(x2d, gpre, w_up, w_down, gpost)


def _rope_tables(seq_len):
    rows = seq_len // GRID_W
    row = np.repeat(np.arange(rows), GRID_W).astype(np.float64)
    col = np.tile(np.arange(GRID_W), rows).astype(np.float64)
    inv = ROPE_THETA ** (-(2.0 * np.arange(ROPE_PAIRS)) / ROPE_AXIS_DIM)
    ang_r = row[:, None] * inv
    ang_c = col[:, None] * inv
    cos = np.concatenate([np.cos(ang_r)] * 2 + [np.cos(ang_c)] * 2, axis=-1)
    sin = np.concatenate([-np.sin(ang_r), np.sin(ang_r), -np.sin(ang_c), np.sin(ang_c)], axis=-1)
    return jnp.asarray(cos, F32), jnp.asarray(sin, F32)


def _row(v):
    return v.reshape(1, -1).astype(F32)


def _encoder_layer(x, mem, p, biases, late_w):
    b, s, d = x.shape
    x2d = x.reshape(b * s, d)
    cast = late_w["w_up"].dtype != BF16
    cos_tab, sin_tab = _rope_tables(s)
    proj = _in_proj(x2d, p["pre_mix_norm"], p["w_in"], p["head_gain"], cos_tab, sin_tab, seq=s,
                    casts=(late_w["w_up"], late_w["w_mem_kv"]) if cast else ())
    pa = proj[0].reshape(b, s, PA_COLS)
    pb = proj[1].reshape(b, s, PB_COLS)
    w_mem_kv = proj[3] if cast else late_w["w_mem_kv"]
    kv = _mem_kv(mem.reshape(-1, d), p["mem_norm"], w_mem_kv).reshape(b, mem.shape[1], -1)

    attn = _attn_am(pa, kv, cast=late_w["w_out"] if cast else None)
    oa, om = attn[0], attn[1]
    ob = _attn_b(pa, pb, biases)
    w_out = attn[2] if cast else late_w["w_out"]

    x1 = _out_proj(oa.reshape(b * s, WIDTH_A), ob.reshape(b * s, WIDTH_B), om.reshape(b * s, WIDTH_M), x2d,
                   p["out_norm_a"], p["out_norm_b"], p["out_norm_m"], w_out, p["post_mix_norm"],
                   cast=late_w["w_down"] if cast else None)
    if cast:
        x1, w_down = x1
        late_w = {"w_mem_kv": w_mem_kv, "w_out": w_out, "w_up": proj[2], "w_down": w_down}
    y = _ffn(x1, p["pre_ffn_norm"], late_w["w_up"], late_w["w_down"], p["post_ffn_norm"])
    return y.reshape(b, s, d), late_w


def kernel(x_prompt, x_sample, mem_prompt, mem_sample, rel_bias, pre_mix_norm, w_in, q_norm_a, k_norm_a, mem_norm, w_mem_kv, out_norm_a, out_norm_b, out_norm_m, w_out, post_mix_norm, pre_ffn_norm, w_up, w_down, post_ffn_norm):
    depth = w_in.shape[0]
    assert x_prompt.shape[1] == x_sample.shape[1]
    seq = x_prompt.shape[1]
    biases = _band_biases(rel_bias, seq)
    layers = []
    for l in range(depth):
        layers.append({
            "pre_mix_norm": _row(pre_mix_norm[l]),
            "w_in": w_in[l].astype(BF16),
            "head_gain": _row(jnp.concatenate([jnp.tile(q_norm_a[l], N_HEADS_A) * Q_SCALE,
                                               jnp.tile(k_norm_a[l], N_KV_A)])),
            "mem_norm": _row(mem_norm[l]),
            "out_norm_a": _row(out_norm_a[l]),
            "out_norm_b": _row(out_norm_b[l]),
            "out_norm_m": _row(out_norm_m[l]),
            "post_mix_norm": _row(post_mix_norm[l]),
            "pre_ffn_norm": _row(pre_ffn_norm[l]),
            "post_ffn_norm": _row(post_ffn_norm[l]),
        })

    xp, xs = x_prompt, x_sample
    for l, p in enumerate(layers):
        late_w = {"w_mem_kv": w_mem_kv[l], "w_out": w_out[l], "w_up": w_up[l], "w_down": w_down[l]}
        xp, late_w = _encoder_layer(xp, mem_prompt, p, biases, late_w)
        xs, _ = _encoder_layer(xs, mem_sample, p, biases, late_w)
    return (xp, xs)
```
